```python
import jax
import jax.numpy as jnp
from jax import lax
import numpy as np

D_MODEL = 1024
BATCH = 8
SEQ = 4096
DEPTH = 1
DEC_BATCH = 32
DEC_SEQ = 1
PAST_LEN = 16384
PAGE_SIZE = 128

N_HEADS = 16
HEAD_DIM = D_MODEL // N_HEADS
N_KV_HEADS = 4
GROUP = N_HEADS // N_KV_HEADS
KV_W = N_KV_HEADS * HEAD_DIM
D_CONV = D_MODEL
CONV_WIDTH = 31
D_FF = 4 * D_MODEL
CMP_STRIDE = 16
CMP_LEN = 2 * CMP_STRIDE
SLC_BLOCK = 64
SLC_RATIO = SLC_BLOCK // CMP_STRIDE
N_SEL = 16
WINDOW = 512
NSA_Q_BLOCK = 16
EPS = 1e-6
IN_SIZES = (2 * D_CONV, N_HEADS * HEAD_DIM) + (KV_W,) * 6 + (3 * N_HEADS, D_MODEL, D_MODEL)
IN_OFFSETS = tuple(int(v) for v in np.cumsum(IN_SIZES)[:-1])
D_IN = sum(IN_SIZES)
F32 = jnp.float32

kernel_name = "hybrid_conformer_nsa_decode_step"


def rms_norm(x, g):
    xf = x.astype(F32)
    y = xf * lax.rsqrt(jnp.mean(xf * xf, axis=-1, keepdims=True) + EPS)
    return (y * g.astype(F32)).astype(x.dtype)


def layer_norm(x, g, b):
    xf = x.astype(F32)
    mu = jnp.mean(xf, axis=-1, keepdims=True)
    var = jnp.mean(jnp.square(xf - mu), axis=-1, keepdims=True)
    y = (xf - mu) * lax.rsqrt(var + EPS)
    return (y * g.astype(F32) + b.astype(F32)).astype(x.dtype)


def alibi_slopes():
    h = np.arange(1, N_HEADS + 1, dtype=np.float32)
    return jnp.asarray(np.power(np.float32(2.0), -8.0 * h / N_HEADS), dtype=F32)


def masked_softmax(s, valid):
    s = jnp.where(valid, s, -jnp.inf)
    m = jnp.max(s, axis=-1, keepdims=True)
    m = jnp.where(jnp.isfinite(m), m, 0.0)
    e = jnp.exp(s - m)
    return e / jnp.maximum(jnp.sum(e, axis=-1, keepdims=True), 1e-30)


def pre_mix(x, ln1_g, w_in):
    b, t, _ = x.shape
    z = rms_norm(x, ln1_g) @ w_in
    u2, q, kc, vc, ks, vs, kw, vw, ng, g_conv, g_attn = jnp.split(z, IN_OFFSETS, axis=-1)
    u = u2[..., :D_CONV] * jax.nn.sigmoid(u2[..., D_CONV:])
    kvh = lambda a: a.reshape(b, t, N_KV_HEADS, HEAD_DIM)
    cmp_kv = jnp.stack([kvh(kc), kvh(vc)], axis=2)
    slc_kv = jnp.stack([kvh(ks), kvh(vs)], axis=2)
    win_kv = jnp.stack([kvh(kw), kvh(vw)], axis=2)
    q = q.reshape(b, t, N_HEADS, HEAD_DIM)
    ng = ng.reshape(b, t, 3, N_HEADS)
    return u, q, cmp_kv, slc_kv, win_kv, ng, g_conv, g_attn


def conv_module(u_ext, conv_w, conv_b, ln_g, ln_b, w_pw):
    c = u_ext.shape[-1]
    y = lax.conv_general_dilated(u_ext, conv_w.astype(u_ext.dtype)[:, None, :], (1,), 'VALID',
                                 dimension_numbers=('NWC', 'WIO', 'NWC'), feature_group_count=c)
    y = jax.nn.silu(layer_norm(y + conv_b.astype(y.dtype), ln_g, ln_b))
    return y @ w_pw


def post_mix(x, conv_out, attn_out, g_conv, g_attn, w_out, ln2_g, w_up, w_down):
    m = jax.nn.sigmoid(g_conv) * conv_out + jax.nn.sigmoid(g_attn) * attn_out.astype(conv_out.dtype)
    x = x + m @ w_out
    h = rms_norm(x, ln2_g)
    return x + jnp.square(jax.nn.relu(h @ w_up)) @ w_down


def compress_kv(pieces, cmp_pos, cmp_w, cmp_w1, cmp_w2):
    b = pieces[0].shape[0]
    t = sum(p.shape[1] for p in pieces)
    nc = -(-t // CMP_STRIDE)
    pad = (nc + 1) * CMP_STRIDE - t
    zeros = jnp.zeros((b, pad, 2, N_KV_HEADS, HEAD_DIM), F32)
    kvp = jnp.concatenate([p.astype(F32) for p in pieces] + [zeros], axis=1)
    ch = kvp.reshape(b, nc + 1, CMP_STRIDE, 2, N_KV_HEADS, HEAD_DIM)
    w = cmp_w.astype(F32)
    first = jnp.einsum('bcpjhd,pj->bcjhd', ch[:, :nc], w[:CMP_STRIDE])
    second = jnp.einsum('bcpjhd,pj->bcjhd', ch[:, 1:], w[CMP_STRIDE:])
    pe = jnp.einsum('pjd,pj->jd', cmp_pos.astype(F32), w)
    pooled = first + second + pe[None, None, :, None, :]
    hid = jax.nn.silu(jnp.einsum('bcjhd,jde->bcjhe', pooled, cmp_w1.astype(F32)))
    comp = jnp.einsum('bcjhe,jef->bcjhf', hid, cmp_w2.astype(F32))
    end = jnp.arange(nc, dtype=jnp.int32) * CMP_STRIDE + (CMP_LEN - 1)
    return comp[:, :, 0], comp[:, :, 1], end


def nsa_attend(q, t_pos, cmp_k, cmp_v, cmp_end, nsb, gather, win_k, win_v, win_pos, gates, slopes):
    b, _, _, nq, _ = q.shape
    qf = q.astype(F32) * (HEAD_DIM ** -0.5)
    tq = t_pos.astype(F32)
    sl = slopes.reshape(N_KV_HEADS, GROUP, 1, 1)
    s_c = jnp.einsum('bkgqd,bckd->bkgqc', qf, cmp_k.astype(F32))
    s_c = s_c - sl * (tq[:, None] - cmp_end[None, :].astype(F32))
    p_c = masked_softmax(s_c, cmp_end[None, :] <= t_pos[:, None])
    o_c = jnp.einsum('bkgqc,bckd->bkgqd', p_c, cmp_v.astype(F32))
    imp = jnp.sum(p_c, axis=2)
    nc = imp.shape[-1]
    imp = jnp.pad(imp, ((0, 0), (0, 0), (0, 0), (1, SLC_RATIO * nsb + SLC_RATIO - 1 - nc)))
    head = imp[..., :SLC_RATIO * nsb].reshape(b, N_KV_HEADS, nq, nsb, SLC_RATIO)
    nxt = imp[..., SLC_RATIO:].reshape(b, N_KV_HEADS, nq, nsb, SLC_RATIO)[..., 0]
    score = head[..., 0] + 2.0 * jnp.sum(head[..., 1:], axis=-1) + nxt
    blk = jnp.arange(nsb, dtype=jnp.int32)[None, :]
    cur = (t_pos // SLC_BLOCK)[:, None]
    forced = (blk == 0) | (blk == cur) | (blk == cur - 1)
    score = jnp.where(forced, jnp.inf, jnp.where(blk > cur, -jnp.inf, score))
    _, idx = lax.top_k(score, min(N_SEL, nsb))
    n_top = idx.shape[-1]
    sk, sv = gather(idx)
    pos = idx[..., None] * SLC_BLOCK + jnp.arange(SLC_BLOCK, dtype=jnp.int32)
    dist_s = (t_pos[:, None, None] - pos)[:, :, None]
    s_s = jnp.einsum('bkgqd,bkqnsd->bkgqns', qf, sk.astype(F32))
    s_s = s_s - slopes.reshape(N_KV_HEADS, GROUP, 1, 1, 1) * dist_s.astype(F32)
    flat = n_top * SLC_BLOCK
    p_s = masked_softmax(s_s.reshape(b, N_KV_HEADS, GROUP, nq, flat),
                         (dist_s >= 0).reshape(b, N_KV_HEADS, 1, nq, flat))
    o_s = jnp.einsum('bkgqns,bkqnsd->bkgqd', p_s.reshape(s_s.shape), sv.astype(F32))
    s_w = jnp.einsum('bkgqd,blkd->bkgql', qf, win_k.astype(F32))
    dist_w = t_pos[:, None] - win_pos[None, :]
    valid_w = (dist_w >= 0) & (dist_w <= WINDOW) & (win_pos[None, :] >= 0)
    s_w = s_w - sl * dist_w.astype(F32)
    p_w = masked_softmax(s_w, valid_w)
    o_w = jnp.einsum('bkgql,blkd->bkgqd', p_w, win_v.astype(F32))
    g = jax.nn.sigmoid(gates.astype(F32)).reshape(b, nq, 3, N_KV_HEADS, GROUP).transpose(2, 0, 3, 4, 1)[..., None]
    out = g[0] * o_c + g[1] * o_s + g[2] * o_w
    return out.astype(q.dtype)


def nsa_prompt(q, cmp_kv, slc_kv, win_kv, gates, cmp_pos, cmp_w, cmp_w1, cmp_w2, slopes):
    b, t = q.shape[:2]
    ck, cv, cend = compress_kv([cmp_kv], cmp_pos, cmp_w, cmp_w1, cmp_w2)
    nsb = -(-t // SLC_BLOCK)
    sblk = jnp.pad(slc_kv, ((0, 0), (0, nsb * SLC_BLOCK - t), (0, 0), (0, 0), (0, 0)))
    sblk = sblk.reshape(b, nsb, SLC_BLOCK, 2, N_KV_HEADS, HEAD_DIM)
    bb = jnp.arange(b)[:, None, None, None]
    hh = jnp.arange(N_KV_HEADS)[None, :, None, None]

    def gather(idx):
        g = sblk[bb, idx, :, :, hh]
        return g[..., 0, :], g[..., 1, :]

    wpad = jnp.pad(win_kv, ((0, 0), (WINDOW, 0), (0, 0), (0, 0), (0, 0)))
    qg = q.reshape(b, t, N_KV_HEADS, GROUP, HEAD_DIM).transpose(0, 2, 3, 1, 4)
    n_win = WINDOW + NSA_Q_BLOCK

    def one_block(i):
        s0 = i * NSA_Q_BLOCK
        qb = lax.dynamic_slice_in_dim(qg, s0, NSA_Q_BLOCK, axis=3)
        gb = lax.dynamic_slice_in_dim(gates, s0, NSA_Q_BLOCK, axis=1)
        wb = lax.dynamic_slice_in_dim(wpad, s0, n_win, axis=1)
        t_pos = s0 + jnp.arange(NSA_Q_BLOCK, dtype=jnp.int32)
        win_pos = s0 - WINDOW + jnp.arange(n_win, dtype=jnp.int32)
        return nsa_attend(qb, t_pos, ck, cv, cend, nsb, gather, wb[:, :, 0], wb[:, :, 1], win_pos, gb, slopes)

    out = lax.map(one_block, jnp.arange(t // NSA_Q_BLOCK, dtype=jnp.int32))
    return out.transpose(1, 0, 4, 2, 3, 5).reshape(b, t, N_HEADS * HEAD_DIM)


def nsa_sample(q, cmp_kv, slc_kv, win_kv, gates, layer, cache_cmp_kv, cache_slc_kv, win_state, page_table,
               cmp_pos, cmp_w, cmp_w1, cmp_w2, slopes):
    b, s = q.shape[:2]
    n_pages = page_table.shape[1]
    past = n_pages * PAGE_SIZE
    past_cmp = cache_cmp_kv[layer, page_table].reshape(b, past, 2, N_KV_HEADS, HEAD_DIM)
    ck, cv, cend = compress_kv([past_cmp, cmp_kv], cmp_pos, cmp_w, cmp_w1, cmp_w2)
    total = past + s
    nsb = -(-total // SLC_BLOCK)
    nb_past = past // SLC_BLOCK
    nb_new = nsb - nb_past
    bpp = PAGE_SIZE // SLC_BLOCK
    new_rows = jnp.pad(slc_kv.astype(cache_slc_kv.dtype),
                       ((0, 0), (0, nb_new * SLC_BLOCK - s), (0, 0), (0, 0), (0, 0)))
    bb = jnp.arange(b)[:, None, None, None]
    hh = jnp.arange(N_KV_HEADS)[None, :, None, None]
    offs = jnp.arange(SLC_BLOCK, dtype=jnp.int32)

    def gather(idx):
        lp = jnp.minimum(idx, nb_past - 1)
        page = page_table[bb, lp // bpp]
        rows = ((lp % bpp) * SLC_BLOCK)[..., None] + offs
        g_past = cache_slc_kv[layer, page[..., None], rows, :, hh[..., None]]
        rows_new = (jnp.clip(idx - nb_past, 0, nb_new - 1) * SLC_BLOCK)[..., None] + offs
        g_new = new_rows[bb[..., None], rows_new, :, hh[..., None]]
        g = jnp.where((idx < nb_past)[..., None, None, None], g_past, g_new)
        return g[..., 0, :], g[..., 1, :]

    n_buf = win_state.shape[1]
    win_all = jnp.concatenate([win_state, win_kv.astype(win_state.dtype)], axis=1)
    win_pos = past - n_buf + jnp.arange(n_buf + s, dtype=jnp.int32)
    t_pos = past + jnp.arange(s, dtype=jnp.int32)
    qg = q.reshape(b, s, N_KV_HEADS, GROUP, HEAD_DIM).transpose(0, 2, 3, 1, 4)
    out = nsa_attend(qg, t_pos, ck, cv, cend, nsb, gather, win_all[:, :, 0], win_all[:, :, 1], win_pos, gates, slopes)
    out = out.transpose(0, 3, 1, 2, 4).reshape(b, s, N_HEADS * HEAD_DIM)
    return out, win_all[:, -n_buf:]


def setup_inputs(seed: int = 0) -> dict:
    key = jax.random.key(seed)
    k = jax.random.split(key, 24)
    n_pages = PAST_LEN // PAGE_SIZE
    n_phys = (DEC_BATCH * n_pages * 5) // 4

    def nrm(kk, shape, scale):
        return jax.random.normal(kk, shape, F32) * scale

    kv_row = (2, N_KV_HEADS, HEAD_DIM)
    x_prompt = nrm(k[0], (BATCH, SEQ, D_MODEL), 1.0)
    x_sample = nrm(k[1], (DEC_BATCH, DEC_SEQ, D_MODEL), 1.0)
    cache_cmp_kv = nrm(k[2], (DEPTH, n_phys, PAGE_SIZE) + kv_row, 1.0)
    cache_slc_kv = nrm(k[3], (DEPTH, n_phys, PAGE_SIZE) + kv_row, 1.0)
    state_win_kv = nrm(k[4], (DEPTH, DEC_BATCH, min(WINDOW, PAST_LEN)) + kv_row, 1.0)
    state_conv = nrm(k[5], (DEPTH, DEC_BATCH, CONV_WIDTH - 1, D_CONV), 0.5)
    page_table = jax.random.permutation(k[6], n_phys)[: DEC_BATCH * n_pages].reshape(DEC_BATCH, n_pages).astype(jnp.int32)
    ln1_g = 1.0 + nrm(k[7], (DEPTH, D_MODEL), 0.02)
    w_in = nrm(k[8], (DEPTH, D_MODEL, D_IN), D_MODEL ** -0.5)
    cmp_pos = nrm(k[9], (DEPTH, CMP_LEN, 2, HEAD_DIM), 0.1)
    cmp_w = (1.0 + nrm(k[10], (DEPTH, CMP_LEN, 2), 0.1)) * (CMP_LEN ** -0.5)
    cmp_w1 = nrm(k[11], (DEPTH, 2, HEAD_DIM, HEAD_DIM), HEAD_DIM ** -0.5)
    cmp_w2 = nrm(k[12], (DEPTH, 2, HEAD_DIM, HEAD_DIM), HEAD_DIM ** -0.5)
    conv_w = nrm(k[13], (DEPTH, CONV_WIDTH, D_CONV), CONV_WIDTH ** -0.5)
    conv_b = nrm(k[14], (DEPTH, D_CONV), 0.01)
    conv_ln_g = 1.0 + nrm(k[15], (DEPTH, D_CONV), 0.02)
    conv_ln_b = nrm(k[16], (DEPTH, D_CONV), 0.01)
    w_conv_pw = nrm(k[17], (DEPTH, D_CONV, D_MODEL), D_CONV ** -0.5)
    w_out = nrm(k[18], (DEPTH, N_HEADS * HEAD_DIM, D_MODEL), D_MODEL ** -0.5)
    ln2_g = 1.0 + nrm(k[19], (DEPTH, D_MODEL), 0.02)
    w_up = nrm(k[20], (DEPTH, D_MODEL, D_FF), D_MODEL ** -0.5)
    w_down = nrm(k[21], (DEPTH, D_FF, D_MODEL), D_FF ** -0.5)
    lnf_g = 1.0 + nrm(k[22], (D_MODEL,), 0.02)
    return {"x_prompt": x_prompt, "x_sample": x_sample, "cache_cmp_kv": cache_cmp_kv,
            "cache_slc_kv": cache_slc_kv, "state_win_kv": state_win_kv, "state_conv": state_conv,
            "page_table": page_table, "ln1_g": ln1_g, "w_in": w_in, "cmp_pos": cmp_pos, "cmp_w": cmp_w,
            "cmp_w1": cmp_w1, "cmp_w2": cmp_w2, "conv_w": conv_w, "conv_b": conv_b, "conv_ln_g": conv_ln_g,
            "conv_ln_b": conv_ln_b, "w_conv_pw": w_conv_pw, "w_out": w_out, "ln2_g": ln2_g, "w_up": w_up,
            "w_down": w_down, "lnf_g": lnf_g}


def reference(x_prompt, x_sample, cache_cmp_kv, cache_slc_kv, state_win_kv, state_conv, page_table,
              ln1_g, w_in, cmp_pos, cmp_w, cmp_w1, cmp_w2, conv_w, conv_b, conv_ln_g, conv_ln_b,
              w_conv_pw, w_out, ln2_g, w_up, w_down, lnf_g):
    slopes = alibi_slopes()
    hp, hs = x_prompt, x_sample
    cmp_p, cmp_s, slc_p, slc_s, win_p, win_s, conv_p, conv_s = [], [], [], [], [], [], [], []
    for l in range(DEPTH):
        u, q, ckv, skv, wkv, ng, g_conv, g_attn = pre_mix(hp, ln1_g[l], w_in[l])
        u_ext = jnp.pad(u, ((0, 0), (CONV_WIDTH - 1, 0), (0, 0)))
        c_out = conv_module(u_ext, conv_w[l], conv_b[l], conv_ln_g[l], conv_ln_b[l], w_conv_pw[l])
        a_out = nsa_prompt(q, ckv, skv, wkv, ng, cmp_pos[l], cmp_w[l], cmp_w1[l], cmp_w2[l], slopes)
        hp = post_mix(hp, c_out, a_out, g_conv, g_attn, w_out[l], ln2_g[l], w_up[l], w_down[l])
        cmp_p.append(ckv)
        slc_p.append(skv)
        win_p.append(wkv[:, -min(WINDOW, wkv.shape[1]):])
        conv_p.append(u_ext[:, -(CONV_WIDTH - 1):])
        u, q, ckv, skv, wkv, ng, g_conv, g_attn = pre_mix(hs, ln1_g[l], w_in[l])
        u_ext = jnp.concatenate([state_conv[l].astype(u.dtype), u], axis=1)
        c_out = conv_module(u_ext, conv_w[l], conv_b[l], conv_ln_g[l], conv_ln_b[l], w_conv_pw[l])
        a_out, win_new = nsa_sample(q, ckv, skv, wkv, ng, l, cache_cmp_kv, cache_slc_kv, state_win_kv[l],
                                    page_table, cmp_pos[l], cmp_w[l], cmp_w1[l], cmp_w2[l], slopes)
        hs = post_mix(hs, c_out, a_out, g_conv, g_attn, w_out[l], ln2_g[l], w_up[l], w_down[l])
        cmp_s.append(ckv)
        slc_s.append(skv)
        win_s.append(win_new)
        conv_s.append(u_ext[:, -(CONV_WIDTH - 1):])
    y_prompt = rms_norm(hp, lnf_g)
    y_sample = rms_norm(hs, lnf_g)
    return (y_prompt, y_sample, jnp.stack(cmp_p), jnp.stack(cmp_s), jnp.stack(slc_p), jnp.stack(slc_s),
            jnp.stack(win_p), jnp.stack(win_s), jnp.stack(conv_p), jnp.stack(conv_s))
```

```python
import functools

import numpy as np
import jax
import jax.numpy as jnp
from jax import lax
from jax.experimental import pallas as pl
from jax.experimental.pallas import tpu as pltpu

F32 = jnp.float32
BF16 = jnp.bfloat16
I32 = jnp.int32
HIGHEST = lax.Precision.HIGHEST

D_MODEL = 1024
N_HEADS = 16
HEAD_DIM = 64
N_KV_HEADS = 4
GROUP = N_HEADS // N_KV_HEADS
D_CONV = D_MODEL
CONV_WIDTH = 31
D_FF = 4 * D_MODEL
CMP_STRIDE = 16
CMP_LEN = 2 * CMP_STRIDE
SLC_BLOCK = 64
SLC_RATIO = SLC_BLOCK // CMP_STRIDE
N_SEL = 16
WINDOW = 512
PAGE_SIZE = 128
EPS = 1e-6
KVW = 2 * N_KV_HEADS * HEAD_DIM
LANES = 128
PAIR_W = 2 * HEAD_DIM
NEG = -1e30
SCORE_FORCED = 3e38
SCORE_BLOCKED = -1e38
SCORE_TAKEN = -3e38
VMEM_LIMIT = 56 * 1024 * 1024

TQ = 256
TK = 256
N_FEAT = 6
PAGES_PER_STEP = 8


def _cparams(sem):
    return pltpu.CompilerParams(dimension_semantics=sem, vmem_limit_bytes=VMEM_LIMIT)


def _alibi_slopes():
    h = np.arange(1, N_HEADS + 1, dtype=np.float32)
    return np.power(np.float32(2.0), -8.0 * h / N_HEADS).astype(np.float32)


def _bf16_parts(x):
    parts = []
    r = np.asarray(x, np.float32)
    for _ in range(3):
        p = r.astype(BF16).astype(np.float32)
        parts.append(p)
        r = (r - p).astype(np.float32)
    return parts


def _sigmoid(x):
    return 1.0 / (1.0 + jnp.exp(-x))


def _silu(x):
    return x * _sigmoid(x)


def _premix_kernel(x_ref, g_ref, wm_ref, wg_ref, u_ref, q_ref, ckv_ref, skv_ref, wkv_ref, ng_ref, gc_ref, ga_ref):
    x = x_ref[...]
    r = lax.rsqrt(jnp.mean(x * x, axis=-1, keepdims=True) + EPS)
    h = (x * r * g_ref[...]).astype(BF16)

    def seg(lo, hi):
        return jnp.dot(h, wm_ref[:, lo:hi], preferred_element_type=F32)

    u2 = seg(0, 2 * D_CONV)
    u_ref[...] = u2[:, :D_CONV] * _sigmoid(u2[:, D_CONV:])
    o = 2 * D_CONV
    q_ref[...] = seg(o, o + D_MODEL)
    o += D_MODEL
    ckv_ref[...] = seg(o, o + KVW)
    skv_ref[...] = seg(o + KVW, o + 2 * KVW)
    wkv_ref[...] = seg(o + 2 * KVW, o + 3 * KVW)
    o += 3 * KVW
    gc_ref[...] = _sigmoid(seg(o, o + D_MODEL))
    ga_ref[...] = _sigmoid(seg(o + D_MODEL, o + 2 * D_MODEL))
    ng_ref[...] = _sigmoid(jnp.dot(h, wg_ref[...], preferred_element_type=F32))


def _premix(x2d, ln_g, wm, wg, tm):
    n = x2d.shape[0]
    row = lambda w: pl.BlockSpec((tm, w), lambda i: (i, 0))
    full = lambda a: pl.BlockSpec(a.shape, lambda i: (0,) * a.ndim, pipeline_mode=pl.Buffered(1))
    widths = (D_CONV, D_MODEL, KVW, KVW, KVW, LANES, D_MODEL, D_MODEL)
    return pl.pallas_call(
        _premix_kernel,
        grid=(n // tm,),
        in_specs=[row(D_MODEL), full(ln_g), full(wm), full(wg)],
        out_specs=[row(w) for w in widths],
        out_shape=[jax.ShapeDtypeStruct((n, w), F32) for w in widths],
        compiler_params=_cparams(("parallel",)),
        name="premix",
    )(x2d, ln_g, wm, wg)


def _pool_pe(wl_ref, pos_ref):
    return jnp.sum(pos_ref[...] * wl_ref[...], axis=0, keepdims=True)


def _cmp_mlp(pooled, w1k_ref, w2k_ref, w1v_ref, w2v_ref):
    half = KVW // 2
    hk = _silu(jnp.dot(pooled[:, :half], w1k_ref[...], precision=HIGHEST, preferred_element_type=F32))
    ck = jnp.dot(hk, w2k_ref[...], precision=HIGHEST, preferred_element_type=F32)
    hv = _silu(jnp.dot(pooled[:, half:], w1v_ref[...], precision=HIGHEST, preferred_element_type=F32))
    cv = jnp.dot(hv, w2v_ref[...], precision=HIGHEST, preferred_element_type=F32)
    return ck, cv


N_LG = KVW // LANES


def _pool_chunks(lane_refs, wl_ref, n_chunks):
    firsts, seconds = [], []
    for j, ref in enumerate(lane_refs):
        lo = j * LANES
        first = jnp.zeros((n_chunks, LANES), F32)
        second = jnp.zeros((n_chunks, LANES), F32)
        for p in range(CMP_STRIDE):
            xp = ref[0, pl.ds(p, n_chunks, stride=CMP_STRIDE), :]
            first = first + wl_ref[p:p + 1, lo:lo + LANES] * xp
            second = second + wl_ref[CMP_STRIDE + p:CMP_STRIDE + p + 1, lo:lo + LANES] * xp
        firsts.append(first)
        seconds.append(second)
    return jnp.concatenate(firsts, axis=1), jnp.concatenate(seconds, axis=1)


def _compress_kernel(*refs, nc):
    lane_refs = refs[:N_LG]
    wl_ref, pos_ref, w1k_ref, w2k_ref, w1v_ref, w2v_ref, ckt_ref, cv_ref = refs[N_LG:]
    first, second = _pool_chunks(lane_refs, wl_ref, nc)
    nxt = pltpu.roll(second, nc - 1, axis=0)
    rows = lax.broadcasted_iota(I32, (nc, 1), 0)
    nxt = jnp.where(rows == nc - 1, 0.0, nxt)
    pooled = first + nxt + _pool_pe(wl_ref, pos_ref)
    ck, cv = _cmp_mlp(pooled, w1k_ref, w2k_ref, w1v_ref, w2v_ref)
    for p in range(2):
        ckt_ref[0, p] = ck[:, p * PAIR_W:(p + 1) * PAIR_W].T
        cv_ref[0, p] = cv[:, p * PAIR_W:(p + 1) * PAIR_W]


def _compress(ckv3, wl, posl, bd, nc):
    b, t, _ = ckv3.shape
    full = lambda a: pl.BlockSpec(a.shape, lambda i: (0,) * a.ndim)
    lane_spec = lambda j: pl.BlockSpec((1, t, LANES), lambda i: (i, 0, j))
    return pl.pallas_call(
        functools.partial(_compress_kernel, nc=nc),
        grid=(b,),
        in_specs=[lane_spec(j) for j in range(N_LG)] + [full(wl), full(posl)] + [full(w) for w in bd],
        out_specs=[pl.BlockSpec((1, 2, PAIR_W, nc), lambda i: (i, 0, 0, 0)),
                   pl.BlockSpec((1, 2, nc, PAIR_W), lambda i: (i, 0, 0, 0))],
        out_shape=[jax.ShapeDtypeStruct((b, 2, PAIR_W, nc), F32), jax.ShapeDtypeStruct((b, 2, nc, PAIR_W), F32)],
        compiler_params=_cparams(("parallel",)),
        name="compress",
    )(*([ckv3] * N_LG), wl, posl, *bd)


def _prep_kernel(x_ref, kt_ref, v_ref, *, pad_tiles, onehot):
    ti = pl.program_id(1)
    x = x_ref[...]
    if pad_tiles:
        x = jnp.where(ti >= pad_tiles, x, 0.0)
    col = ti * TK + lax.broadcasted_iota(I32, (SLC_BLOCK, TK), 1)
    row = lax.broadcasted_iota(I32, (SLC_BLOCK, TK), 0)
    blk = col // SLC_BLOCK
    off = col % SLC_BLOCK
    feat = jnp.where(row < 3, blk, jnp.where(row < N_FEAT, off, 0)).astype(F32).astype(BF16)
    if onehot:
        oh = jnp.where(row == blk, 1.0, 0.0).astype(BF16)
    else:
        oh = jnp.zeros((SLC_BLOCK, TK), BF16)
    half = KVW // 2
    for p in range(2):
        kt_ref[0, p, 0, 0:PAIR_W, :] = x[:, p * PAIR_W:(p + 1) * PAIR_W].T.astype(BF16)
        kt_ref[0, p, 0, PAIR_W:PAIR_W + SLC_BLOCK, :] = oh
        kt_ref[0, p, 0, PAIR_W + SLC_BLOCK:, :] = feat
        v_ref[0, p] = x[:, half + p * PAIR_W:half + (p + 1) * PAIR_W].astype(BF16)


def _prep(kv2d, b, t, pad_tiles, onehot):
    nt = t // TK
    return pl.pallas_call(
        functools.partial(_prep_kernel, pad_tiles=pad_tiles, onehot=onehot),
        grid=(b, nt + pad_tiles),
        in_specs=[pl.BlockSpec((TK, KVW), lambda i, j: (i * nt + jnp.maximum(j - pad_tiles, 0), 0))],
        out_specs=[pl.BlockSpec((1, 2, 1, 2 * PAIR_W, TK), lambda i, j: (i, 0, j, 0, 0)),
                   pl.BlockSpec((1, 2, TK, PAIR_W), lambda i, j: (i, 0, j, 0))],
        out_shape=[jax.ShapeDtypeStruct((b, 2, nt + pad_tiles, 2 * PAIR_W, TK), BF16),
                   jax.ShapeDtypeStruct((b, 2, (nt + pad_tiles) * TK, PAIR_W), BF16)],
        compiler_params=_cparams(("parallel", "parallel")),
        name="prep_onehot" if onehot else "prep_window",
    )(kv2d)


def _top_mask(score, n_pick):
    w = score.shape[-1]
    idx = lax.broadcasted_iota(I32, score.shape, score.ndim - 1).astype(F32)
    taken = jnp.zeros(score.shape, F32)
    firsts = []
    for _ in range(n_pick):
        m = jnp.max(score, axis=-1, keepdims=True)
        first = jnp.min(jnp.where(score == m, idx, float(w)), axis=-1, keepdims=True)
        pick = idx == first
        taken = jnp.where(pick, 1.0, taken)
        score = jnp.where(pick, SCORE_TAKEN, score)
        firsts.append(first)
    return taken > 0.5, firsts


def _attn_kernel(slopes_ref, q_ref, ng_ref, ckt_ref, cv_ref, kts_ref, vs_ref, ktw_ref, vw_ref, amat_ref, qfeat_ref,
                 o_ref, q8_ref, m_ref, l_ref, acc_ref, oc_ref, ow_ref, *, nc, nsb):
    pair = pl.program_id(1)
    qi = pl.program_id(2)
    t0 = qi * TQ
    hp = 2 * GROUP
    lane = lax.broadcasted_iota(I32, (TQ, LANES), 1)
    tq = t0 + lax.broadcasted_iota(I32, (TQ, 1), 0)
    cend = lax.broadcasted_iota(I32, (1, nc), 1) * CMP_STRIDE + (CMP_LEN - 1)
    cvalid = cend <= tq
    cend_f = cend.astype(F32)

    imp = [jnp.zeros((TQ, nc), F32), jnp.zeros((TQ, nc), F32)]
    for g in range(hp):
        kh = g // GROUP
        blk = q_ref[:, LANES * (g // 2):LANES * (g // 2 + 1)] * (HEAD_DIM ** -0.5)
        if g % 2 != kh:
            blk = pltpu.roll(blk, HEAD_DIM, axis=1)
        own = (lane >= HEAD_DIM) if kh else (lane < HEAD_DIM)
        qpad = jnp.where(own, blk, 0.0)
        q8_ref[g * TQ:(g + 1) * TQ, 0:LANES] = qpad.astype(BF16)
        s = jnp.dot(qpad, ckt_ref[0, 0], precision=HIGHEST, preferred_element_type=F32)
        s = s + slopes_ref[pair * hp + g] * cend_f
        s = jnp.where(cvalid, s, NEG)
        m = jnp.max(s, axis=-1, keepdims=True)
        e = jnp.where(cvalid, jnp.exp(s - m), 0.0)
        p = e / jnp.maximum(jnp.sum(e, axis=-1, keepdims=True), 1e-30)
        oc_ref[g * TQ:(g + 1) * TQ, :] = jnp.dot(p.astype(BF16), cv_ref[0, 0].astype(BF16), preferred_element_type=F32)
        imp[kh] = imp[kh] + p

    sblk = lax.broadcasted_iota(I32, (1, SLC_BLOCK), 1)
    cur = tq // SLC_BLOCK
    forced = (sblk == 0) | (sblk == cur) | (sblk == cur - 1)
    for kh in range(2):
        sc = jnp.dot(imp[kh], amat_ref[...], precision=HIGHEST, preferred_element_type=F32)
        sc = jnp.where(forced, SCORE_FORCED, jnp.where(sblk > cur, SCORE_BLOCKED, sc))
        sel, _ = _top_mask(sc, min(N_SEL, nsb))
        bias = jnp.where(sel & (sblk <= cur), 0.0, NEG)
        for g in range(kh * GROUP, (kh + 1) * GROUP):
            feat = jnp.broadcast_to(qfeat_ref[0, g:g + 1, :], (TQ, SLC_BLOCK))
            q8_ref[g * TQ:(g + 1) * TQ, LANES:2 * LANES] = jnp.concatenate([bias, feat], axis=1).astype(BF16)

    rows = hp * TQ
    tq8 = t0 + (lax.broadcasted_iota(I32, (rows, 1), 0) & (TQ - 1))

    def online(s, v):
        m_prev = m_ref[...]
        m_new = jnp.maximum(m_prev, jnp.max(s, axis=-1, keepdims=True))
        alpha = jnp.exp(m_prev - m_new)
        p = jnp.exp(s - m_new)
        l_ref[...] = alpha * l_ref[...] + jnp.sum(p, axis=-1, keepdims=True)
        acc_ref[...] = alpha * acc_ref[...] + jnp.dot(p.astype(BF16), v, preferred_element_type=F32)
        m_ref[...] = m_new

    m_ref[...] = jnp.full((rows, 1), NEG, F32)
    l_ref[...] = jnp.zeros((rows, 1), F32)
    acc_ref[...] = jnp.zeros((rows, PAIR_W), F32)

    def body(kt, carry):
        k0 = pl.multiple_of(kt * TK, TK)
        s = jnp.dot(q8_ref[...], kts_ref[0, 0, kt], preferred_element_type=F32)
        online(s, vs_ref[0, 0, pl.ds(k0, TK), :])
        return carry

    lax.fori_loop(0, qi, body, 0)
    k0 = pl.multiple_of(qi * TK, TK)
    s = jnp.dot(q8_ref[...], kts_ref[0, 0, qi], preferred_element_type=F32)
    kpos = k0 + lax.broadcasted_iota(I32, (1, TK), 1)
    s = jnp.where(kpos <= tq8, s, NEG)
    online(s, vs_ref[0, 0, pl.ds(k0, TK), :])
    o_s = acc_ref[...] / l_ref[...]

    nwt = WINDOW // TK + 1
    kpos = t0 - WINDOW + lax.broadcasted_iota(I32, (1, nwt * TK), 1)
    wvalid = (kpos <= tq) & (kpos >= tq - WINDOW) & (kpos >= 0)

    def wbody(g, carry):
        r0 = pl.multiple_of(g * TQ, TQ)
        qg = q8_ref[pl.ds(r0, TQ), :]
        s = jnp.concatenate([jnp.dot(qg, ktw_ref[0, 0, qi + i], preferred_element_type=F32) for i in range(nwt)], axis=1)
        s = jnp.where(wvalid, s, NEG)
        m = jnp.max(s, axis=-1, keepdims=True)
        e = jnp.exp(s - m)
        l = jnp.sum(e, axis=-1, keepdims=True)
        v = vw_ref[0, 0, pl.ds(pl.multiple_of(t0, TK), nwt * TK), :]
        ow_ref[pl.ds(r0, TQ), :] = jnp.dot(e.astype(BF16), v, preferred_element_type=F32) / l
        return carry

    lax.fori_loop(0, hp, wbody, 0)

    ng = ng_ref[...]
    outs = []
    for g in range(hp):
        kh = g // GROUP
        h = pair * hp + g
        gate = [jnp.sum(jnp.where(lane == j * N_HEADS + h, ng, 0.0), axis=-1, keepdims=True) for j in range(3)]
        r = slice(g * TQ, (g + 1) * TQ)
        og = gate[0] * oc_ref[r, :] + gate[1] * o_s[r, :] + gate[2] * ow_ref[r, :]
        if g % 2 != kh:
            og = pltpu.roll(og, HEAD_DIM, axis=1)
        outs.append(og)
    for j in range(hp // 2):
        o_ref[:, j * LANES:(j + 1) * LANES] = jnp.where(lane < HEAD_DIM, outs[2 * j], outs[2 * j + 1])


def _attention(slopes, q2d, ng, ckt, cv, kts, vs, ktw, vw, amat, qfeat, b, t, nc, nsb):
    nq = t // TQ
    hp = 2 * GROUP
    rows = hp * TQ
    pw = hp * HEAD_DIM
    grid_spec = pltpu.PrefetchScalarGridSpec(
        num_scalar_prefetch=0,
        grid=(b, 2, nq),
        in_specs=[
            pl.BlockSpec(memory_space=pltpu.SMEM),
            pl.BlockSpec((TQ, pw), lambda i, p, j: (i * nq + j, p)),
            pl.BlockSpec((TQ, LANES), lambda i, p, j: (i * nq + j, 0)),
            pl.BlockSpec((1, 1, PAIR_W, nc), lambda i, p, j: (i, p, 0, 0)),
            pl.BlockSpec((1, 1, nc, PAIR_W), lambda i, p, j: (i, p, 0, 0)),
            pl.BlockSpec((1, 1) + kts.shape[2:], lambda i, p, j: (i, p, 0, 0, 0)),
            pl.BlockSpec((1, 1) + vs.shape[2:], lambda i, p, j: (i, p, 0, 0)),
            pl.BlockSpec((1, 1) + ktw.shape[2:], lambda i, p, j: (i, p, 0, 0, 0)),
            pl.BlockSpec((1, 1) + vw.shape[2:], lambda i, p, j: (i, p, 0, 0)),
            pl.BlockSpec(amat.shape, lambda i, p, j: (0, 0)),
            pl.BlockSpec((1, hp, SLC_BLOCK), lambda i, p, j: (p, 0, 0)),
        ],
        out_specs=pl.BlockSpec((TQ, pw), lambda i, p, j: (i * nq + j, p)),
        scratch_shapes=[
            pltpu.VMEM((rows, 2 * LANES), BF16),
            pltpu.VMEM((rows, 1), F32),
            pltpu.VMEM((rows, 1), F32),
            pltpu.VMEM((rows, PAIR_W), F32),
            pltpu.VMEM((rows, PAIR_W), F32),
            pltpu.VMEM((rows, PAIR_W), F32),
        ],
    )
    return pl.pallas_call(
        functools.partial(_attn_kernel, nc=nc, nsb=nsb),
        grid_spec=grid_spec,
        out_shape=jax.ShapeDtypeStruct((b * t, N_HEADS * HEAD_DIM), F32),
        compiler_params=_cparams(("parallel", "parallel", "arbitrary")),
        name="attn",
    )(slopes, q2d, ng, ckt, cv, kts, vs, ktw, vw, amat, qfeat)


HALO = 32


def _ln_silu_pw(y, lg_ref, lb_ref, wpw_ref):
    mu = jnp.mean(y, axis=-1, keepdims=True)
    d = y - mu
    var = jnp.mean(d * d, axis=-1, keepdims=True)
    z = d * lax.rsqrt(var + EPS) * lg_ref[...] + lb_ref[...]
    return jnp.dot(_silu(z).astype(BF16), wpw_ref[...], preferred_element_type=F32)


def _conv_kernel(prev_ref, cur_ref, cw_ref, cb_ref, lg_ref, lb_ref, wpw_ref, o_ref, ext_ref, *, tm):
    first = pl.program_id(1) == 0
    ext_ref[0:HALO, :] = jnp.where(first, 0.0, prev_ref[...])
    ext_ref[HALO:, :] = cur_ref[...]
    base = HALO - (CONV_WIDTH - 1)
    y = jnp.zeros((tm, D_CONV), F32)
    for k in range(CONV_WIDTH):
        y = y + cw_ref[k:k + 1, :] * ext_ref[base + k:base + k + tm, :]
    o_ref[...] = _ln_silu_pw(y + cb_ref[...], lg_ref, lb_ref, wpw_ref)


def _conv_prompt(u2d, cw, cb, lg, lb, wpw, b, t, tm):
    nt = t // tm
    full = lambda a: pl.BlockSpec(a.shape, lambda i, j: (0,) * a.ndim)
    return pl.pallas_call(
        functools.partial(_conv_kernel, tm=tm),
        grid=(b, nt),
        in_specs=[pl.BlockSpec((HALO, D_CONV), lambda i, j: (jnp.maximum((i * nt + j) * (tm // HALO) - 1, 0), 0)),
                  pl.BlockSpec((tm, D_CONV), lambda i, j: (i * nt + j, 0)),
                  full(cw), full(cb), full(lg), full(lb), full(wpw)],
        out_specs=pl.BlockSpec((tm, D_MODEL), lambda i, j: (i * nt + j, 0)),
        out_shape=jax.ShapeDtypeStruct((b * t, D_MODEL), F32),
        scratch_shapes=[pltpu.VMEM((HALO + tm, D_CONV), F32)],
        compiler_params=_cparams(("parallel", "arbitrary")),
        name="conv",
    )(u2d, u2d, cw, cb, lg, lb, wpw)


def _conv_sample_kernel(st_ref, u_ref, cw_ref, cb_ref, lg_ref, lb_ref, wpw_ref, o_ref):
    y = cw_ref[CONV_WIDTH - 1:CONV_WIDTH, :] * u_ref[...]
    for k in range(CONV_WIDTH - 1):
        y = y + cw_ref[k:k + 1, :] * st_ref[k]
    o_ref[...] = _ln_silu_pw(y + cb_ref[...], lg_ref, lb_ref, wpw_ref)


def _conv_sample(state_t, u, cw, cb, lg, lb, wpw):
    args = (state_t, u, cw, cb, lg, lb, wpw)
    return pl.pallas_call(
        _conv_sample_kernel,
        out_shape=jax.ShapeDtypeStruct(u.shape, F32),
        compiler_params=pltpu.CompilerParams(vmem_limit_bytes=VMEM_LIMIT),
        name="conv_sample",
    )(*args)


def _post_kernel(x_ref, c_ref, a_ref, gc_ref, ga_ref, wo_ref, g2_ref, wu_ref, wd_ref, gf_ref, y_ref):
    m = gc_ref[...] * c_ref[...] + ga_ref[...] * a_ref[...]
    x1 = x_ref[...] + jnp.dot(m.astype(BF16), wo_ref[...], preferred_element_type=F32)
    r = lax.rsqrt(jnp.mean(x1 * x1, axis=-1, keepdims=True) + EPS)
    h = (x1 * r * g2_ref[...]).astype(BF16)
    f = jnp.maximum(jnp.dot(h, wu_ref[...], preferred_element_type=F32), 0.0)
    x2 = x1 + jnp.dot((f * f).astype(BF16), wd_ref[...], preferred_element_type=F32)
    r2 = lax.rsqrt(jnp.mean(x2 * x2, axis=-1, keepdims=True) + EPS)
    y_ref[...] = x2 * r2 * gf_ref[...]


def _post(x2d, conv, attn, gc, ga, wo, g2, wu, wd, gf, tm):
    n = x2d.shape[0]
    row = pl.BlockSpec((tm, D_MODEL), lambda i: (i, 0))
    full = lambda a: pl.BlockSpec(a.shape, lambda i: (0,) * a.ndim, pipeline_mode=pl.Buffered(1))
    return pl.pallas_call(
        _post_kernel,
        grid=(n // tm,),
        in_specs=[row] * 5 + [full(wo), full(g2), full(wu), full(wd), full(gf)],
        out_specs=row,
        out_shape=jax.ShapeDtypeStruct((n, D_MODEL), F32),
        compiler_params=_cparams(("parallel",)),
        name="post",
    )(x2d, conv, attn, gc, ga, wo, g2, wu, wd, gf)


def _pool_stream_kernel(pt_ref, *refs):
    npg = PAGES_PER_STEP
    pages = refs[:npg * N_LG]
    wl_ref, out_ref, lastf_ref, carry_ref = refs[npg * N_LG:]
    cpp = PAGE_SIZE // CMP_STRIDE

    @pl.when(pl.program_id(1) == 0)
    def _():
        carry_ref[...] = jnp.zeros(carry_ref.shape, F32)

    prev = carry_ref[0:1, :]
    rows = lax.broadcasted_iota(I32, (cpp, 1), 0)
    for j in range(npg):
        first, second = _pool_chunks(pages[j * N_LG:(j + 1) * N_LG], wl_ref, cpp)
        shifted = jnp.where(rows == 0, prev, pltpu.roll(first, 1, axis=0))
        out_ref[0, j * cpp:(j + 1) * cpp, :] = shifted + second
        prev = first[cpp - 1:cpp, :]
    carry_ref[0:1, :] = prev
    lastf_ref[0] = jnp.broadcast_to(prev, (8, KVW))


def _pool_stream(page_table, cache3, wl):
    b, n_pages = page_table.shape
    npg = PAGES_PER_STEP
    cpp = PAGE_SIZE // CMP_STRIDE
    page_spec = lambda j, lg: pl.BlockSpec((1, PAGE_SIZE, LANES), lambda i, g, pt: (pt[i, g * npg + j], 0, lg))
    grid_spec = pltpu.PrefetchScalarGridSpec(
        num_scalar_prefetch=1,
        grid=(b, n_pages // npg),
        in_specs=[page_spec(j, lg) for j in range(npg) for lg in range(N_LG)]
                 + [pl.BlockSpec(wl.shape, lambda i, g, pt: (0, 0))],
        out_specs=[pl.BlockSpec((1, npg * cpp, KVW), lambda i, g, pt: (i, g, 0)),
                   pl.BlockSpec((1, 8, KVW), lambda i, g, pt: (i, 0, 0))],
        scratch_shapes=[pltpu.VMEM((8, KVW), F32)],
    )
    return pl.pallas_call(
        _pool_stream_kernel,
        grid_spec=grid_spec,
        out_shape=[jax.ShapeDtypeStruct((b, n_pages * cpp, KVW), F32), jax.ShapeDtypeStruct((b, 8, KVW), F32)],
        compiler_params=_cparams(("parallel", "arbitrary")),
        name="pool_stream",
    )(page_table, *([cache3] * (npg * N_LG)), wl)


def _head_rows_q(qrep):
    rep = qrep * (HEAD_DIM ** -0.5)
    lane_kv = lax.broadcasted_iota(I32, rep.shape, 1) // HEAD_DIM
    row_kv = lax.broadcasted_iota(I32, rep.shape, 0) // GROUP
    return jnp.where(lane_kv == row_kv, rep, 0.0), lane_kv == row_kv


def _fold_groups(x):
    return x[:, 0:64] + x[:, 64:128] + x[:, 128:192] + x[:, 192:256]


def _s_cmp_kernel(ps_ref, lastf_ref, new_ref, q_ref, wl_ref, pos_ref, w1k_ref, w2k_ref, w1v_ref, w2v_ref, amat_ref,
                  slopes_ref, oc_ref, idx_ref, *, past):
    nch = past // CMP_STRIDE
    r = nch + 8
    new = new_ref[0]
    trow = lax.broadcasted_iota(I32, (8, 1), 0)
    tail = jnp.where(trow == 0, lastf_ref[0, 0:1, :] + wl_ref[CMP_STRIDE:CMP_STRIDE + 1, :] * new,
                     jnp.where(trow == 1, wl_ref[0:1, :] * new, 0.0))
    pooled = jnp.concatenate([ps_ref[0], tail], axis=0) + _pool_pe(wl_ref, pos_ref)
    ck, cv = _cmp_mlp(pooled, w1k_ref, w2k_ref, w1v_ref, w2v_ref)
    qpad, live = _head_rows_q(q_ref[0])
    s = lax.dot_general(qpad, ck, (((1,), (1,)), ((), ())), precision=HIGHEST, preferred_element_type=F32)
    j = lax.broadcasted_iota(I32, (1, r), 1)
    cend = (j * CMP_STRIDE + (CMP_LEN - 1 - CMP_STRIDE))
    valid = (j >= 1) & (cend <= past)
    s = s - slopes_ref[...] * (past - cend).astype(F32)
    s = jnp.where(valid, s, NEG)
    m = jnp.max(s, axis=-1, keepdims=True)
    e = jnp.where(valid, jnp.exp(s - m), 0.0)
    p = e / jnp.maximum(jnp.sum(e, axis=-1, keepdims=True), 1e-30)
    o = jnp.dot(p.astype(BF16), cv.astype(BF16), preferred_element_type=F32)
    oc_ref[0] = _fold_groups(jnp.where(live, o, 0.0))
    gi = lax.broadcasted_iota(I32, (8, N_HEADS), 0)
    gh = lax.broadcasted_iota(I32, (8, N_HEADS), 1) // GROUP
    imp = jnp.dot(jnp.where(gi == gh, 1.0, 0.0), p, precision=HIGHEST, preferred_element_type=F32)
    sc = jnp.dot(imp, amat_ref[...], precision=HIGHEST, preferred_element_type=F32)
    cur = past // SLC_BLOCK
    sb = lax.broadcasted_iota(I32, sc.shape, 1)
    sc = jnp.where((sb >= 1) & (sb <= cur - 2), sc, SCORE_TAKEN)
    _, firsts = _top_mask(sc, N_SEL - 3)
    lane = lax.broadcasted_iota(I32, (8, LANES), 1)
    idx = jnp.where(lane == 1, cur - 1, 0)
    for n, f in enumerate(firsts):
        idx = jnp.where(lane == 2 + n, f.astype(I32), idx)
    idx_ref[0] = idx


def _s_cmp(ps, lastf, new_ckv, q3, wl, posl, bd, amat, slopes, past):
    b = ps.shape[0]
    full = lambda a: pl.BlockSpec(a.shape, lambda i: (0,) * a.ndim)
    per = lambda a: pl.BlockSpec((1,) + a.shape[1:], lambda i: (i,) + (0,) * (a.ndim - 1))
    return pl.pallas_call(
        functools.partial(_s_cmp_kernel, past=past),
        grid=(b,),
        in_specs=[per(ps), per(lastf), per(new_ckv), per(q3), full(wl), full(posl)] + [full(w) for w in bd]
                 + [full(amat), full(slopes)],
        out_specs=[pl.BlockSpec((1, N_HEADS, HEAD_DIM), lambda i: (i, 0, 0)), pl.BlockSpec((1, 8, LANES), lambda i: (i, 0, 0))],
        out_shape=[jax.ShapeDtypeStruct((b, N_HEADS, HEAD_DIM), F32), jax.ShapeDtypeStruct((b, 8, LANES), I32)],
        compiler_params=_cparams(("parallel",)),
        name="sample_cmp",
    )(ps, lastf, new_ckv, q3, wl, posl, *bd, amat, slopes)


N_PAST_SEL = N_SEL - 1


def _s_slc_kernel(idx_ref, pt_ref, *refs, past):
    blocks = refs[:N_PAST_SEL]
    new_ref, q_ref, slope_ref, o_ref = refs[N_PAST_SEL:]
    i = pl.program_id(0)
    kvh = pl.program_id(1)
    half = KVW // 2
    kst = jnp.concatenate([blk[0, :, 0:half] for blk in blocks], axis=0)
    vst = jnp.concatenate([blk[0, :, half:] for blk in blocks], axis=0)
    rep = q_ref[0, 0] * (HEAD_DIM ** -0.5)
    live = (lax.broadcasted_iota(I32, rep.shape, 1) // HEAD_DIM) == kvh
    qpad = jnp.where(live, rep, 0.0)
    s = lax.dot_general(qpad.astype(BF16), kst.astype(BF16), (((1,), (1,)), ((), ())), preferred_element_type=F32)
    nk = N_PAST_SEL * SLC_BLOCK
    col = lax.broadcasted_iota(I32, (1, nk), 1)
    pos = col % SLC_BLOCK
    for n in range(N_PAST_SEL):
        pos = pos + jnp.where(col // SLC_BLOCK == n, idx_ref[(i * N_KV_HEADS + kvh) * N_SEL + n] * SLC_BLOCK, 0)
    slope = slope_ref[0]
    s = s - slope * (past - pos).astype(F32)
    new = new_ref[0]
    s_new = jnp.sum(qpad * new[:, 0:half], axis=-1, keepdims=True)
    m = jnp.maximum(jnp.max(s, axis=-1, keepdims=True), s_new)
    e = jnp.exp(s - m)
    e_new = jnp.exp(s_new - m)
    l = jnp.sum(e, axis=-1, keepdims=True) + e_new
    o = jnp.dot(e.astype(BF16), vst.astype(BF16), preferred_element_type=F32) + e_new * new[:, half:]
    o_ref[0, 0] = jnp.where(live, o / l, 0.0)


def _s_slc(idx, page_table, cache_blocks, new_skv, q4, slopes4, past):
    b, n_pages = page_table.shape
    bpp = PAGE_SIZE // SLC_BLOCK
    idx = idx.reshape(-1)
    page_table = page_table.reshape(-1)

    def blk_spec(n):
        def imap(i, k, idx_r, pt_r):
            lb = idx_r[(i * N_KV_HEADS + k) * N_SEL + n]
            return (pt_r[i * n_pages + lb // bpp] * bpp + lb % bpp, 0, 0)
        return pl.BlockSpec((1, SLC_BLOCK, KVW), imap)

    grid_spec = pltpu.PrefetchScalarGridSpec(
        num_scalar_prefetch=2,
        grid=(b, N_KV_HEADS),
        in_specs=[blk_spec(n) for n in range(N_PAST_SEL)] + [
            pl.BlockSpec((1, 1, KVW), lambda i, k, a, c: (i, 0, 0)),
            pl.BlockSpec((1, 1, GROUP, KVW // 2), lambda i, k, a, c: (i, k, 0, 0)),
            pl.BlockSpec((1, GROUP, 1), lambda i, k, a, c: (k, 0, 0)),
        ],
        out_specs=pl.BlockSpec((1, 1, GROUP, KVW // 2), lambda i, k, a, c: (i, k, 0, 0)),
    )
    return pl.pallas_call(
        functools.partial(_s_slc_kernel, past=past),
        grid_spec=grid_spec,
        out_shape=jax.ShapeDtypeStruct((b, N_KV_HEADS, GROUP, KVW // 2), F32),
        compiler_params=_cparams(("parallel", "arbitrary")),
        name="sample_slc",
    )(idx, page_table, *([cache_blocks] * N_PAST_SEL), new_skv, q4, slopes4)


def _s_win_kernel(st_ref, new_ref, q_ref, oc_ref, os_ref, g_ref, slopes_ref, o_ref):
    half = KVW // 2
    st = st_ref[0]
    n_buf = st.shape[0]
    qpad, live = _head_rows_q(q_ref[0])
    s = lax.dot_general(qpad.astype(BF16), st[:, 0:half].astype(BF16), (((1,), (1,)), ((), ())), preferred_element_type=F32)
    dist = n_buf - lax.broadcasted_iota(I32, (1, n_buf), 1)
    s = s - slopes_ref[...] * dist.astype(F32)
    new = new_ref[0]
    s_new = jnp.sum(qpad * new[:, 0:half], axis=-1, keepdims=True)
    m = jnp.maximum(jnp.max(s, axis=-1, keepdims=True), s_new)
    e = jnp.exp(s - m)
    e_new = jnp.exp(s_new - m)
    l = jnp.sum(e, axis=-1, keepdims=True) + e_new
    o = jnp.dot(e.astype(BF16), st[:, half:].astype(BF16), preferred_element_type=F32) + e_new * new[:, half:]
    o_w = _fold_groups(jnp.where(live, o / l, 0.0))
    o_s = _fold_groups(os_ref[0])
    g = g_ref[0]
    o_ref[0] = g[:, 0:1] * oc_ref[0] + g[:, 1:2] * o_s + g[:, 2:3] * o_w


def _s_win(state, new_wkv, q3, oc, os16, gates, slopes):
    b = state.shape[0]
    per = lambda a: pl.BlockSpec((1,) + a.shape[1:], lambda i: (i,) + (0,) * (a.ndim - 1))
    return pl.pallas_call(
        _s_win_kernel,
        grid=(b,),
        in_specs=[per(state), per(new_wkv), per(q3), per(oc), per(os16), per(gates),
                  pl.BlockSpec(slopes.shape, lambda i: (0, 0))],
        out_specs=pl.BlockSpec((1, N_HEADS, HEAD_DIM), lambda i: (i, 0, 0)),
        out_shape=jax.ShapeDtypeStruct((b, N_HEADS, HEAD_DIM), F32),
        compiler_params=_cparams(("parallel",)),
        name="sample_win",
    )(state, new_wkv, q3, oc, os16, gates, slopes)


def _overlap_matrix(n_rows, n_cols, nsb, row_of_block0):
    a = np.zeros((n_rows, n_cols), np.float32)
    wts = (1.0, 2.0, 2.0, 2.0, 1.0)
    for sb in range(nsb):
        for d, w in enumerate(wts):
            r = SLC_RATIO * sb - 1 + d + row_of_block0
            if row_of_block0 <= r < n_rows:
                a[r, sb] = w
    return a


def _q_features():
    s = _alibi_slopes()
    parts = _bf16_parts(s)
    f = np.zeros((N_HEADS, SLC_BLOCK), np.float32)
    for i, p in enumerate(parts):
        f[:, i] = p * SLC_BLOCK
        f[:, 3 + i] = p
    return f.reshape(2, 2 * GROUP, SLC_BLOCK)


def _block_diag(w):
    return jnp.kron(jnp.eye(N_KV_HEADS, dtype=F32), w.astype(F32))


def kernel(x_prompt, x_sample, cache_cmp_kv, cache_slc_kv, state_win_kv, state_conv, page_table, ln1_g, w_in, cmp_pos,
           cmp_w, cmp_w1, cmp_w2, conv_w, conv_b, conv_ln_g, conv_ln_b, w_conv_pw, w_out, ln2_g, w_up, w_down, lnf_g):
    b, t, _ = x_prompt.shape
    bs, s_new, _ = x_sample.shape
    depth = ln1_g.shape[0]
    n_pages = page_table.shape[1]
    past = n_pages * PAGE_SIZE
    n_buf = state_win_kv.shape[2]
    assert depth == 1 and s_new == 1
    assert t % TQ == 0 and t // SLC_BLOCK <= SLC_BLOCK and t >= WINDOW and WINDOW % TK == 0
    assert n_pages % PAGES_PER_STEP == 0 and past // SLC_BLOCK >= N_SEL and n_buf == WINDOW and past >= WINDOW
    l = 0
    nc = t // CMP_STRIDE
    nsb = t // SLC_BLOCK
    slopes = _alibi_slopes()

    o_ng = 2 * D_CONV + D_MODEL + 3 * KVW
    wi = w_in[l]
    wm = jnp.concatenate([wi[:, :o_ng], wi[:, o_ng + 3 * N_HEADS:]], axis=1).astype(BF16)
    wg = jnp.pad(wi[:, o_ng:o_ng + 3 * N_HEADS], ((0, 0), (0, LANES - 3 * N_HEADS))).astype(BF16)
    wpw = w_conv_pw[l].astype(BF16)
    wo = w_out[l].astype(BF16)
    wu = w_up[l].astype(BF16)
    wd = w_down[l].astype(BF16)
    row = lambda v: v.reshape(1, -1).astype(F32)
    cw = jnp.pad(conv_w[l].astype(F32), ((0, 1), (0, 0)))
    lane_j = np.repeat(np.arange(2), KVW // 2)
    wl = cmp_w[l].astype(F32)[:, lane_j]
    posl = jnp.tile(cmp_pos[l].astype(F32)[:, :, None, :], (1, 1, N_KV_HEADS, 1)).reshape(CMP_LEN, KVW)
    bd = [_block_diag(cmp_w1[l, 0]), _block_diag(cmp_w2[l, 0]), _block_diag(cmp_w1[l, 1]), _block_diag(cmp_w2[l, 1])]

    x2d = x_prompt.reshape(b * t, D_MODEL)
    u, q, ckv, skv, wkv, ng, gc, ga = _premix(x2d, row(ln1_g[l]), wm, wg, 256)
    ckt, cv = _compress(ckv.reshape(b, t, KVW), wl, posl, bd, nc)
    kts, vs = _prep(skv, b, t, 0, True)
    ktw, vw = _prep(wkv, b, t, WINDOW // TK, False)
    amat = jnp.asarray(_overlap_matrix(nc, SLC_BLOCK, nsb, 0))
    attn = _attention(jnp.asarray(slopes), q, ng, ckt, cv, kts, vs, ktw, vw, amat, jnp.asarray(_q_features()), b, t, nc, nsb)
    conv = _conv_prompt(u, cw, row(conv_b[l]), row(conv_ln_g[l]), row(conv_ln_b[l]), wpw, b, t, 256)
    y_prompt = _post(x2d, conv, attn, gc, ga, wo, row(ln2_g[l]), wu, wd, row(lnf_g), 256).reshape(b, t, D_MODEL)

    kv_shape = (2, N_KV_HEADS, HEAD_DIM)
    cmp_p = ckv.reshape((1, b, t) + kv_shape)
    slc_p = skv.reshape((1, b, t) + kv_shape)
    win_p = wkv.reshape((b, t) + kv_shape)[None, :, t - min(WINDOW, t):]
    conv_p = u.reshape(b, t, D_CONV)[None, :, t - (CONV_WIDTH - 1):]

    xs2d = x_sample.reshape(bs, D_MODEL)
    us, qs, ckv_s, skv_s, wkv_s, ng_s, gc_s, ga_s = _premix(xs2d, row(ln1_g[l]), wm, wg, bs)
    st_conv = state_conv[l].astype(F32)
    conv_s = _conv_sample(jnp.transpose(st_conv, (1, 0, 2)), us, cw, row(conv_b[l]), row(conv_ln_g[l]),
                          row(conv_ln_b[l]), wpw)
    n_phys = cache_cmp_kv.shape[1]
    ps, lastf = _pool_stream(page_table, cache_cmp_kv[l].reshape(n_phys, PAGE_SIZE, KVW), wl)
    nch = past // CMP_STRIDE
    nsb_s = past // SLC_BLOCK + 1
    nsbp = -(-nsb_s // LANES) * LANES
    amat_s = jnp.asarray(_overlap_matrix(nch + 8, nsbp, nsb_s, 1))
    q3 = jnp.tile(qs.reshape(bs, N_HEADS, HEAD_DIM), (1, 1, N_KV_HEADS))
    slopes_col = jnp.asarray(slopes.reshape(N_HEADS, 1))
    oc_s, idx = _s_cmp(ps, lastf, ckv_s.reshape(bs, 1, KVW), q3, wl, posl, bd, amat_s, slopes_col, past)
    bpp = PAGE_SIZE // SLC_BLOCK
    os_s = _s_slc(idx[:, :N_KV_HEADS, :N_SEL], page_table, cache_slc_kv[l].reshape(n_phys * bpp, SLC_BLOCK, KVW),
                  skv_s.reshape(bs, 1, KVW), q3.reshape(bs, N_KV_HEADS, GROUP, KVW // 2),
                  jnp.asarray(slopes.reshape(N_KV_HEADS, GROUP, 1)), past)
    gates = jnp.transpose(ng_s[:, :3 * N_HEADS].reshape(bs, 3, N_HEADS), (0, 2, 1))
    win_state = state_win_kv[l].reshape(bs, n_buf, KVW)
    attn_s = _s_win(win_state, wkv_s.reshape(bs, 1, KVW), q3, oc_s, os_s.reshape(bs, N_HEADS, KVW // 2), gates, slopes_col)
    y_sample = _post(xs2d, conv_s, attn_s.reshape(bs, D_MODEL), gc_s, ga_s, wo, row(ln2_g[l]), wu, wd, row(lnf_g), bs)
    y_sample = y_sample.reshape(bs, 1, D_MODEL)

    cmp_s = ckv_s.reshape((1, bs, 1) + kv_shape)
    slc_s = skv_s.reshape((1, bs, 1) + kv_shape)
    win_s = jnp.concatenate([state_win_kv[l], wkv_s.reshape((bs, 1) + kv_shape).astype(state_win_kv.dtype)], axis=1)[None, :, 1:]
    conv_st = jnp.concatenate([st_conv, us[:, None, :]], axis=1)[None, :, 1:]
    return (y_prompt, y_sample, cmp_p, cmp_s, slc_p, slc_s, win_p, win_s, conv_p, conv_st)
```

```python
import functools

import numpy as np
import jax
import jax.numpy as jnp
from jax import lax
from jax.experimental import pallas as pl
from jax.experimental.pallas import tpu as pltpu

F32 = jnp.float32
BF16 = jnp.bfloat16
I32 = jnp.int32
HIGHEST = lax.Precision.HIGHEST

D_MODEL = 1024
N_HEADS = 16
HEAD_DIM = 64
N_KV_HEADS = 4
GROUP = N_HEADS // N_KV_HEADS
D_CONV = D_MODEL
CONV_WIDTH = 31
D_FF = 4 * D_MODEL
CMP_STRIDE = 16
CMP_LEN = 2 * CMP_STRIDE
SLC_BLOCK = 64
SLC_RATIO = SLC_BLOCK // CMP_STRIDE
N_SEL = 16
WINDOW = 512
PAGE_SIZE = 128
EPS = 1e-6
KVW = 2 * N_KV_HEADS * HEAD_DIM
LANES = 128
PAIR_W = 2 * HEAD_DIM
NEG = -1e30
SCORE_FORCED = 3e38
SCORE_BLOCKED = -1e38
SCORE_TAKEN = -3e38
VMEM_LIMIT = 56 * 1024 * 1024

TQ = 256
TK = 256
N_FEAT = 6
PAGES_PER_STEP = 16


def _cparams(sem):
    return pltpu.CompilerParams(dimension_semantics=sem, vmem_limit_bytes=VMEM_LIMIT)


def _alibi_slopes():
    h = np.arange(1, N_HEADS + 1, dtype=np.float32)
    return np.power(np.float32(2.0), -8.0 * h / N_HEADS).astype(np.float32)


def _bf16_parts(x):
    parts = []
    r = np.asarray(x, np.float32)
    for _ in range(3):
        p = r.astype(BF16).astype(np.float32)
        parts.append(p)
        r = (r - p).astype(np.float32)
    return parts


def _sigmoid(x):
    return 1.0 / (1.0 + jnp.exp(-x))


def _silu(x):
    return x * _sigmoid(x)


def _key_tail(pos, lane, onehot):
    blk = pos // SLC_BLOCK
    off = pos % SLC_BLOCK
    f = lane - SLC_BLOCK
    feat = jnp.where((f >= 0) & (f < 3), blk, jnp.where((f >= 3) & (f < N_FEAT), off, 0))
    if onehot:
        feat = jnp.where(lane == blk, 1, feat)
    return feat.astype(F32).astype(BF16)


def _premix_kernel(x_ref, g_ref, wm_ref, wg_ref, u_ref, q_ref, ckv_ref, skv_ref, wkv_ref, ng_ref, gc_ref, ga_ref,
                   *kvt_refs):
    x = x_ref[...]
    r = lax.rsqrt(jnp.mean(x * x, axis=-1, keepdims=True) + EPS)
    h = (x * r * g_ref[...]).astype(BF16)

    def seg(lo, hi):
        return jnp.dot(h, wm_ref[:, lo:hi], preferred_element_type=F32)

    u2 = seg(0, 2 * D_CONV)
    u_ref[...] = u2[:, :D_CONV] * _sigmoid(u2[:, D_CONV:])
    o = 2 * D_CONV
    q_ref[...] = seg(o, o + D_MODEL)
    o += D_MODEL
    for i, kv_ref in enumerate((ckv_ref, skv_ref, wkv_ref)):
        kv = seg(o + i * KVW, o + (i + 1) * KVW)
        kv_ref[...] = kv
        if kvt_refs:
            kvt_refs[i][0] = kv.T
    o += 3 * KVW
    gc_ref[...] = _sigmoid(seg(o, o + D_MODEL))
    ga_ref[...] = _sigmoid(seg(o + D_MODEL, o + 2 * D_MODEL))
    ng_ref[...] = _sigmoid(jnp.dot(h, wg_ref[...], preferred_element_type=F32))


def _premix(x2d, ln_g, wm, wg, tm, seq_len=None):
    n = x2d.shape[0]
    row = lambda w: pl.BlockSpec((tm, w), lambda i: (i, 0))
    full = lambda a: pl.BlockSpec(a.shape, lambda i: (0,) * a.ndim, pipeline_mode=pl.Buffered(1))
    widths = (D_CONV, D_MODEL, KVW, KVW, KVW, LANES, D_MODEL, D_MODEL)
    out_specs = [row(w) for w in widths]
    out_shape = [jax.ShapeDtypeStruct((n, w), F32) for w in widths]
    if seq_len is not None:
        nt = seq_len // tm
        out_specs += [pl.BlockSpec((1, KVW, tm), lambda i: (i // nt, 0, i % nt))] * 3
        out_shape += [jax.ShapeDtypeStruct((n // seq_len, KVW, seq_len), F32)] * 3
    return pl.pallas_call(
        _premix_kernel,
        grid=(n // tm,),
        in_specs=[row(D_MODEL), full(ln_g), full(wm), full(wg)],
        out_specs=out_specs,
        out_shape=out_shape,
        compiler_params=_cparams(("parallel",)),
        name="premix",
    )(x2d, ln_g, wm, wg)


def _pool_pe(wl_ref, pos_ref):
    return jnp.sum(pos_ref[...] * wl_ref[...], axis=0, keepdims=True)


def _cmp_mlp(pooled, w1k_ref, w2k_ref, w1v_ref, w2v_ref):
    half = KVW // 2
    hk = _silu(jnp.dot(pooled[:, :half], w1k_ref[...], precision=HIGHEST, preferred_element_type=F32))
    ck = jnp.dot(hk, w2k_ref[...], precision=HIGHEST, preferred_element_type=F32)
    hv = _silu(jnp.dot(pooled[:, half:], w1v_ref[...], precision=HIGHEST, preferred_element_type=F32))
    cv = jnp.dot(hv, w2v_ref[...], precision=HIGHEST, preferred_element_type=F32)
    return ck, cv


N_LG = KVW // LANES


def _pool_chunks(lane_refs, wl_ref, n_chunks):
    firsts, seconds = [], []
    for j, ref in enumerate(lane_refs):
        lo = j * LANES
        first = jnp.zeros((n_chunks, LANES), F32)
        second = jnp.zeros((n_chunks, LANES), F32)
        for p in range(CMP_STRIDE):
            xp = ref[0, pl.ds(p, n_chunks, stride=CMP_STRIDE), :]
            first = first + wl_ref[p:p + 1, lo:lo + LANES] * xp
            second = second + wl_ref[CMP_STRIDE + p:CMP_STRIDE + p + 1, lo:lo + LANES] * xp
        firsts.append(first)
        seconds.append(second)
    return jnp.concatenate(firsts, axis=1), jnp.concatenate(seconds, axis=1)


def _compress_kernel(*refs, nc):
    lane_refs = refs[:N_LG]
    wl_ref, pos_ref, w1k_ref, w2k_ref, w1v_ref, w2v_ref, cka_ref, cvt_ref = refs[N_LG:]
    first, second = _pool_chunks(lane_refs, wl_ref, nc)
    nxt = pltpu.roll(second, nc - 1, axis=0)
    rows = lax.broadcasted_iota(I32, (nc, 1), 0)
    nxt = jnp.where(rows == nc - 1, 0.0, nxt)
    pooled = first + nxt + _pool_pe(wl_ref, pos_ref)
    ck, cv = _cmp_mlp(pooled, w1k_ref, w2k_ref, w1v_ref, w2v_ref)
    cend = lax.broadcasted_iota(I32, (nc, LANES), 0) * CMP_STRIDE + (CMP_LEN - 1)
    tail = _key_tail(cend, lax.broadcasted_iota(I32, (nc, LANES), 1), False)
    for p in range(2):
        cka_ref[0, p, :, 0:PAIR_W] = ck[:, p * PAIR_W:(p + 1) * PAIR_W].astype(BF16)
        cka_ref[0, p, :, PAIR_W:] = tail
        cvt_ref[0, p] = cv[:, p * PAIR_W:(p + 1) * PAIR_W].T.astype(BF16)


def _compress(ckv3, wl, posl, bd, nc):
    b, t, _ = ckv3.shape
    full = lambda a: pl.BlockSpec(a.shape, lambda i: (0,) * a.ndim)
    lane_spec = lambda j: pl.BlockSpec((1, t, LANES), lambda i: (i, 0, j))
    return pl.pallas_call(
        functools.partial(_compress_kernel, nc=nc),
        grid=(b,),
        in_specs=[lane_spec(j) for j in range(N_LG)] + [full(wl), full(posl)] + [full(w) for w in bd],
        out_specs=[pl.BlockSpec((1, 2, nc, 2 * PAIR_W), lambda i: (i, 0, 0, 0)),
                   pl.BlockSpec((1, 2, PAIR_W, nc), lambda i: (i, 0, 0, 0))],
        out_shape=[jax.ShapeDtypeStruct((b, 2, nc, 2 * PAIR_W), BF16), jax.ShapeDtypeStruct((b, 2, PAIR_W, nc), BF16)],
        compiler_params=_cparams(("parallel",)),
        name="compress",
    )(*([ckv3] * N_LG), wl, posl, *bd)


VT_ROWS = PAIR_W + 16


def _prep_kernel(x_ref, ka_ref, vt_ref, *, onehot):
    ti = pl.program_id(1)
    x = x_ref[...]
    pos = ti * TK + lax.broadcasted_iota(I32, (TK, LANES), 0)
    tail = _key_tail(pos, lax.broadcasted_iota(I32, (TK, LANES), 1), onehot)
    half = KVW // 2
    for p in range(2):
        ka_ref[0, p, :, 0:PAIR_W] = x[:, p * PAIR_W:(p + 1) * PAIR_W].astype(BF16)
        ka_ref[0, p, :, PAIR_W:] = tail
        vt_ref[0, p, 0, 0:PAIR_W, :] = x[:, half + p * PAIR_W:half + (p + 1) * PAIR_W].T.astype(BF16)
        vt_ref[0, p, 0, PAIR_W:, :] = jnp.ones((VT_ROWS - PAIR_W, TK), BF16)


def _prep(kv2d, b, t, onehot):
    nt = t // TK
    return pl.pallas_call(
        functools.partial(_prep_kernel, onehot=onehot),
        grid=(b, nt),
        in_specs=[pl.BlockSpec((TK, KVW), lambda i, j: (i * nt + j, 0))],
        out_specs=[pl.BlockSpec((1, 2, TK, 2 * PAIR_W), lambda i, j: (i, 0, j, 0)),
                   pl.BlockSpec((1, 2, 1, VT_ROWS, TK), lambda i, j: (i, 0, j, 0, 0))],
        out_shape=[jax.ShapeDtypeStruct((b, 2, t, 2 * PAIR_W), BF16),
                   jax.ShapeDtypeStruct((b, 2, nt, VT_ROWS, TK), BF16)],
        compiler_params=_cparams(("parallel", "parallel")),
        name="prep_onehot" if onehot else "prep_window",
    )(kv2d)


def _top_mask(score, n_pick, axis):
    w = score.shape[axis]
    idx = lax.broadcasted_iota(I32, score.shape, axis).astype(F32)
    taken = jnp.zeros(score.shape, F32)
    firsts = []
    for _ in range(n_pick):
        m = jnp.max(score, axis=axis, keepdims=True)
        first = jnp.min(jnp.where(score == m, idx, float(w)), axis=axis, keepdims=True)
        pick = idx == first
        taken = jnp.where(pick, 1.0, taken)
        score = jnp.where(pick, SCORE_TAKEN, score)
        firsts.append(first)
    return taken > 0.5, firsts


def _attn_kernel(q_ref, ng_ref, cka_ref, cvt_ref, ksa_ref, vst_ref, kwa_ref, vwt_ref, qfeat_ref,
                 o_ref, q8_ref, m_ref, acc_ref, oc_ref, os_ref, imp_ref, ngt_ref, *, nc, nsb):
    pair = pl.program_id(1)
    qi = pl.program_id(2)
    t0 = qi * TQ
    hp = 2 * GROUP
    cols = hp * TQ
    nlh = TQ // LANES
    tq8 = t0 + (lax.broadcasted_iota(I32, (1, cols), 1) & (TQ - 1))

    zero_head = jnp.zeros((HEAD_DIM, TQ), F32)
    for j in range(hp // 2):
        qt = q_ref[:, LANES * j:LANES * (j + 1)].T * (HEAD_DIM ** -0.5)
        for g in (2 * j, 2 * j + 1):
            c = slice(g * TQ, (g + 1) * TQ)
            own = qt[HEAD_DIM * (g % 2):HEAD_DIM * (g % 2 + 1), :]
            kpart = [own, zero_head] if g // GROUP == 0 else [zero_head, own]
            q8_ref[0:PAIR_W, c] = jnp.concatenate(kpart, axis=0).astype(BF16)
            q8_ref[PAIR_W:PAIR_W + SLC_BLOCK, c] = jnp.zeros((SLC_BLOCK, TQ), BF16)
            q8_ref[PAIR_W + SLC_BLOCK:, c] = jnp.concatenate([qfeat_ref[0, g]] * nlh, axis=1).astype(BF16)

    s = jnp.dot(cka_ref[0, 0], q8_ref[...], preferred_element_type=F32)
    cend = lax.broadcasted_iota(I32, (nc, 1), 0) * CMP_STRIDE + (CMP_LEN - 1)
    cvalid = cend <= tq8
    s = jnp.where(cvalid, s, NEG)
    m = jnp.max(s, axis=0, keepdims=True)
    e = jnp.where(cvalid, jnp.exp(s - m), 0.0)
    p = e / jnp.maximum(jnp.sum(e, axis=0, keepdims=True), 1e-30)
    oc_ref[...] = jnp.dot(cvt_ref[0, 0], p.astype(BF16), preferred_element_type=F32)
    for kh in range(2):
        imp = p[:, (kh * GROUP) * TQ:(kh * GROUP + 1) * TQ]
        for g in range(kh * GROUP + 1, (kh + 1) * GROUP):
            imp = imp + p[:, g * TQ:(g + 1) * TQ]
        for h in range(nlh):
            imp_ref[kh, h] = imp[:, h * LANES:(h + 1) * LANES]

    sblk = lax.broadcasted_iota(I32, (SLC_BLOCK, 1), 0)
    cur = tq8[:, 0:TQ] // SLC_BLOCK
    forced = (sblk == 0) | (sblk == cur) | (sblk == cur - 1)
    first_row = lax.broadcasted_iota(I32, (nsb, 1), 0) == 0
    for kh in range(2):
        halves = []
        for h in range(nlh):
            x = [imp_ref[kh, h, pl.ds(k, nsb, stride=SLC_RATIO), :] for k in range(SLC_RATIO)]
            prev = jnp.where(first_row, 0.0, pltpu.roll(x[SLC_RATIO - 1], 1, axis=0))
            halves.append(prev + 2.0 * (x[0] + x[1] + x[2]) + x[3])
        sc = jnp.concatenate(halves, axis=1)
        if nsb < SLC_BLOCK:
            sc = jnp.concatenate([sc, jnp.zeros((SLC_BLOCK - nsb, TQ), F32)], axis=0)
        sc = jnp.where(forced, SCORE_FORCED, jnp.where(sblk > cur, SCORE_BLOCKED, sc))
        sel, _ = _top_mask(sc, min(N_SEL, nsb), 0)
        bias = jnp.where(sel & (sblk <= cur), 0.0, NEG).astype(BF16)
        for g in range(kh * GROUP, (kh + 1) * GROUP):
            q8_ref[PAIR_W:PAIR_W + SLC_BLOCK, g * TQ:(g + 1) * TQ] = bias

    def start():
        m_ref[...] = jnp.full(m_ref.shape, NEG, F32)
        acc_ref[...] = jnp.zeros(acc_ref.shape, F32)

    def online(keys, vt, mask):
        s = jnp.dot(keys, q8_ref[...], preferred_element_type=F32)
        if mask is not None:
            s = jnp.where(mask, s, NEG)
        m_prev = m_ref[0:1, :]
        m_new = jnp.maximum(m_prev, jnp.max(s, axis=0, keepdims=True))
        alpha = jnp.exp(m_prev - m_new)
        p = jnp.exp(s - m_new)
        acc_ref[...] = alpha * acc_ref[...] + jnp.dot(vt, p.astype(BF16), preferred_element_type=F32)
        m_ref[0:1, :] = m_new

    def finish():
        return acc_ref[0:PAIR_W, :] / acc_ref[PAIR_W:PAIR_W + 1, :]

    krow = lax.broadcasted_iota(I32, (TK, 1), 0)

    start()

    def body(kt, carry):
        online(ksa_ref[0, 0, pl.ds(pl.multiple_of(kt * TK, TK), TK), :], vst_ref[0, 0, kt], None)
        return carry

    lax.fori_loop(0, qi, body, 0)
    online(ksa_ref[0, 0, pl.ds(pl.multiple_of(t0, TK), TK), :], vst_ref[0, 0, qi], (t0 + krow) <= tq8)
    os_ref[...] = finish()

    start()
    nwt = WINDOW // TK + 1
    for i in range(nwt):
        kt = qi - (nwt - 1) + i
        kpos = kt * TK + krow
        mask = (kpos <= tq8) & (kpos >= tq8 - WINDOW) & (kpos >= 0)
        ktc = jnp.maximum(kt, 0)
        online(kwa_ref[0, 0, pl.ds(pl.multiple_of(ktc * TK, TK), TK), :], vwt_ref[0, 0, ktc], mask)
    o_w = finish()

    ngt_ref[...] = ng_ref[...].T
    outs = []
    for g in range(hp):
        kh = g // GROUP
        h = pair * hp + g
        gate = [ngt_ref[pl.ds(j * N_HEADS + h, 1), :] for j in range(3)]
        r = slice(kh * HEAD_DIM, (kh + 1) * HEAD_DIM)
        c = slice(g * TQ, (g + 1) * TQ)
        outs.append(gate[0] * oc_ref[r, c] + gate[1] * os_ref[r, c] + gate[2] * o_w[r, c])
    o_ref[...] = jnp.concatenate(outs, axis=0).T


def _attention(q2d, ng, cka, cvt, ksa, vst, kwa, vwt, qfeat, b, t, nc, nsb):
    nq = t // TQ
    hp = 2 * GROUP
    cols = hp * TQ
    pw = hp * HEAD_DIM
    per_pair = lambda a: pl.BlockSpec((1, 1) + a.shape[2:], lambda i, p, j: (i, p) + (0,) * (a.ndim - 2))
    return pl.pallas_call(
        functools.partial(_attn_kernel, nc=nc, nsb=nsb),
        grid=(b, 2, nq),
        in_specs=[
            pl.BlockSpec((TQ, pw), lambda i, p, j: (i * nq + j, p)),
            pl.BlockSpec((TQ, LANES), lambda i, p, j: (i * nq + j, 0)),
            per_pair(cka), per_pair(cvt), per_pair(ksa), per_pair(vst), per_pair(kwa), per_pair(vwt),
            pl.BlockSpec((1,) + qfeat.shape[1:], lambda i, p, j: (p, 0, 0, 0)),
        ],
        out_specs=pl.BlockSpec((TQ, pw), lambda i, p, j: (i * nq + j, p)),
        out_shape=jax.ShapeDtypeStruct((b * t, N_HEADS * HEAD_DIM), F32),
        scratch_shapes=[
            pltpu.VMEM((2 * PAIR_W, cols), BF16),
            pltpu.VMEM((8, cols), F32),
            pltpu.VMEM((VT_ROWS, cols), F32),
            pltpu.VMEM((PAIR_W, cols), F32),
            pltpu.VMEM((PAIR_W, cols), F32),
            pltpu.VMEM((2, TQ // LANES, nc, LANES), F32),
            pltpu.VMEM((LANES, TQ), F32),
        ],
        compiler_params=_cparams(("parallel", "parallel", "arbitrary")),
        name="attn",
    )(q2d, ng, cka, cvt, ksa, vst, kwa, vwt, qfeat)


HALO = 32


def _ln_silu_pw(y, lg_ref, lb_ref, wpw_ref):
    mu = jnp.mean(y, axis=-1, keepdims=True)
    d = y - mu
    var = jnp.mean(d * d, axis=-1, keepdims=True)
    z = d * lax.rsqrt(var + EPS) * lg_ref[...] + lb_ref[...]
    return jnp.dot(_silu(z).astype(BF16), wpw_ref[...], preferred_element_type=F32)


def _conv_kernel(prev_ref, cur_ref, cw_ref, cb_ref, lg_ref, lb_ref, wpw_ref, o_ref, ext_ref, *, tm):
    first = pl.program_id(1) == 0
    ext_ref[0:HALO, :] = jnp.where(first, 0.0, prev_ref[...])
    ext_ref[HALO:, :] = cur_ref[...]
    base = HALO - (CONV_WIDTH - 1)
    y = jnp.zeros((tm, D_CONV), F32)
    for k in range(CONV_WIDTH):
        y = y + cw_ref[k:k + 1, :] * ext_ref[base + k:base + k + tm, :]
    o_ref[...] = _ln_silu_pw(y + cb_ref[...], lg_ref, lb_ref, wpw_ref)


def _conv_prompt(u2d, cw, cb, lg, lb, wpw, b, t, tm):
    nt = t // tm
    full = lambda a: pl.BlockSpec(a.shape, lambda i, j: (0,) * a.ndim)
    return pl.pallas_call(
        functools.partial(_conv_kernel, tm=tm),
        grid=(b, nt),
        in_specs=[pl.BlockSpec((HALO, D_CONV), lambda i, j: (jnp.maximum((i * nt + j) * (tm // HALO) - 1, 0), 0)),
                  pl.BlockSpec((tm, D_CONV), lambda i, j: (i * nt + j, 0)),
                  full(cw), full(cb), full(lg), full(lb), full(wpw)],
        out_specs=pl.BlockSpec((tm, D_MODEL), lambda i, j: (i * nt + j, 0)),
        out_shape=jax.ShapeDtypeStruct((b * t, D_MODEL), F32),
        scratch_shapes=[pltpu.VMEM((HALO + tm, D_CONV), F32)],
        compiler_params=_cparams(("parallel", "arbitrary")),
        name="conv",
    )(u2d, u2d, cw, cb, lg, lb, wpw)


def _conv_sample_kernel(st_ref, u_ref, cw_ref, cb_ref, lg_ref, lb_ref, wpw_ref, o_ref):
    y = cw_ref[CONV_WIDTH - 1:CONV_WIDTH, :] * u_ref[...]
    for k in range(CONV_WIDTH - 1):
        y = y + cw_ref[k:k + 1, :] * st_ref[k]
    o_ref[...] = _ln_silu_pw(y + cb_ref[...], lg_ref, lb_ref, wpw_ref)


def _conv_sample(state_t, u, cw, cb, lg, lb, wpw):
    args = (state_t, u, cw, cb, lg, lb, wpw)
    return pl.pallas_call(
        _conv_sample_kernel,
        out_shape=jax.ShapeDtypeStruct(u.shape, F32),
        compiler_params=pltpu.CompilerParams(vmem_limit_bytes=VMEM_LIMIT),
        name="conv_sample",
    )(*args)


def _post_kernel(x_ref, c_ref, a_ref, gc_ref, ga_ref, wo_ref, g2_ref, wu_ref, wd_ref, gf_ref, y_ref):
    m = gc_ref[...] * c_ref[...] + ga_ref[...] * a_ref[...]
    x1 = x_ref[...] + jnp.dot(m.astype(BF16), wo_ref[...], preferred_element_type=F32)
    r = lax.rsqrt(jnp.mean(x1 * x1, axis=-1, keepdims=True) + EPS)
    h = (x1 * r * g2_ref[...]).astype(BF16)
    f = jnp.maximum(jnp.dot(h, wu_ref[...], preferred_element_type=F32), 0.0)
    x2 = x1 + jnp.dot((f * f).astype(BF16), wd_ref[...], preferred_element_type=F32)
    r2 = lax.rsqrt(jnp.mean(x2 * x2, axis=-1, keepdims=True) + EPS)
    y_ref[...] = x2 * r2 * gf_ref[...]


def _post(x2d, conv, attn, gc, ga, wo, g2, wu, wd, gf, tm):
    n = x2d.shape[0]
    row = pl.BlockSpec((tm, D_MODEL), lambda i: (i, 0))
    full = lambda a: pl.BlockSpec(a.shape, lambda i: (0,) * a.ndim, pipeline_mode=pl.Buffered(1))
    return pl.pallas_call(
        _post_kernel,
        grid=(n // tm,),
        in_specs=[row] * 5 + [full(wo), full(g2), full(wu), full(wd), full(gf)],
        out_specs=row,
        out_shape=jax.ShapeDtypeStruct((n, D_MODEL), F32),
        compiler_params=_cparams(("parallel",)),
        name="post",
    )(x2d, conv, attn, gc, ga, wo, g2, wu, wd, gf)


CHUNKS_PER_PAGE = PAGE_SIZE // CMP_STRIDE
STEP_CHUNKS = PAGES_PER_STEP * CHUNKS_PER_PAGE


def _pool_stream_kernel(pt_ref, *refs):
    npg = PAGES_PER_STEP
    pages = refs[:npg]
    mk_ref, mv_ref, out_ref, lastf_ref, carry_ref = refs[npg:]
    half = KVW // 2

    @pl.when(pl.program_id(1) == 0)
    def _():
        carry_ref[...] = jnp.zeros(carry_ref.shape, F32)

    acc_k = jnp.zeros((half, 2 * STEP_CHUNKS), F32)
    acc_v = jnp.zeros((half, 2 * STEP_CHUNKS), F32)
    for g in range(npg):
        x = pages[g][0]
        hi = x.astype(BF16)
        lo = (x - hi.astype(F32)).astype(BF16)
        xx = jnp.concatenate([hi, lo], axis=1)
        acc_k = acc_k + jnp.dot(xx[0:half], mk_ref[g], preferred_element_type=F32)
        acc_v = acc_v + jnp.dot(xx[half:], mv_ref[g], preferred_element_type=F32)
    full = jnp.concatenate([acc_k, acc_v], axis=0)
    lane = lax.broadcasted_iota(I32, (KVW, STEP_CHUNKS), 1)
    out_ref[0] = full[:, 0:STEP_CHUNKS] + jnp.where(lane == 0, carry_ref[...], 0.0)
    carry_ref[...] = full[:, STEP_CHUNKS:]
    lastf_ref[0] = full[:, STEP_CHUNKS:]


def _pool_matrices(w):
    pos = np.arange(2 * PAGE_SIZE) % PAGE_SIZE
    chunk, p = pos // CMP_STRIDE, pos % CMP_STRIDE
    col = np.arange(2 * STEP_CHUNKS)
    mats = []
    for g in range(PAGES_PER_STEP):
        c0 = g * CHUNKS_PER_PAGE + chunk
        second = (col[None, :] == c0[:, None]).astype(np.float32)
        first = (col[None, :] == c0[:, None] + 1).astype(np.float32)
        mats.append(w[CMP_STRIDE + p][:, None] * second + w[p][:, None] * first)
    return jnp.stack(mats).astype(BF16)


def _pool_stream(page_table, cache_t, mk, mv):
    b, n_pages = page_table.shape
    npg = PAGES_PER_STEP
    page_spec = lambda j: pl.BlockSpec((1, KVW, PAGE_SIZE), lambda i, g, pt: (pt[i, g * npg + j], 0, 0))
    const = lambda a: pl.BlockSpec(a.shape, lambda i, g, pt: (0,) * a.ndim)
    grid_spec = pltpu.PrefetchScalarGridSpec(
        num_scalar_prefetch=1,
        grid=(b, n_pages // npg),
        in_specs=[page_spec(j) for j in range(npg)] + [const(mk), const(mv)],
        out_specs=[pl.BlockSpec((1, KVW, STEP_CHUNKS), lambda i, g, pt: (i, 0, g)),
                   pl.BlockSpec((1, KVW, STEP_CHUNKS), lambda i, g, pt: (i, 0, 0))],
        scratch_shapes=[pltpu.VMEM((KVW, STEP_CHUNKS), F32)],
    )
    n_chunks = n_pages * CHUNKS_PER_PAGE
    return pl.pallas_call(
        _pool_stream_kernel,
        grid_spec=grid_spec,
        out_shape=[jax.ShapeDtypeStruct((b, KVW, n_chunks), F32), jax.ShapeDtypeStruct((b, KVW, STEP_CHUNKS), F32)],
        compiler_params=_cparams(("parallel", "arbitrary")),
        name="pool_stream",
    )(page_table, *([cache_t] * npg), mk, mv)


def _head_rows_q(qrep):
    rep = qrep * (HEAD_DIM ** -0.5)
    lane_kv = lax.broadcasted_iota(I32, rep.shape, 1) // HEAD_DIM
    row_kv = lax.broadcasted_iota(I32, rep.shape, 0) // GROUP
    return jnp.where(lane_kv == row_kv, rep, 0.0), lane_kv == row_kv


def _fold_groups(x):
    return x[:, 0:64] + x[:, 64:128] + x[:, 128:192] + x[:, 192:256]


def _s_cmp_kernel(ps_ref, lastf_ref, new_ref, q_ref, wlt_ref, post_ref, w1k_ref, w2k_ref, w1v_ref, w2v_ref, amat_ref,
                  slopes_ref, oc_ref, idx_ref, *, past):
    nch = past // CMP_STRIDE
    r = nch + LANES
    half = KVW // 2
    new = new_ref[0]
    tl = lax.broadcasted_iota(I32, (KVW, LANES), 1)
    tail = jnp.where(tl == 0, lastf_ref[0, :, 0:1] + wlt_ref[:, CMP_STRIDE:CMP_STRIDE + 1] * new,
                     jnp.where(tl == 1, wlt_ref[:, 0:1] * new, 0.0))
    pe = jnp.sum(post_ref[...] * wlt_ref[...], axis=1, keepdims=True)
    pooled = jnp.concatenate([ps_ref[0], tail], axis=1) + pe
    hk = _silu(jnp.dot(w1k_ref[...], pooled[0:half], precision=HIGHEST, preferred_element_type=F32))
    ck = jnp.dot(w2k_ref[...], hk, precision=HIGHEST, preferred_element_type=F32)
    hv = _silu(jnp.dot(w1v_ref[...], pooled[half:], precision=HIGHEST, preferred_element_type=F32))
    cv = jnp.dot(w2v_ref[...], hv, precision=HIGHEST, preferred_element_type=F32)
    qpad, live = _head_rows_q(q_ref[0])
    s = jnp.dot(qpad, ck, precision=HIGHEST, preferred_element_type=F32)
    j = lax.broadcasted_iota(I32, (1, r), 1)
    cend = (j * CMP_STRIDE + (CMP_LEN - 1 - CMP_STRIDE))
    valid = (j >= 1) & (cend <= past)
    s = s - slopes_ref[...] * (past - cend).astype(F32)
    s = jnp.where(valid, s, NEG)
    m = jnp.max(s, axis=-1, keepdims=True)
    e = jnp.where(valid, jnp.exp(s - m), 0.0)
    p = e / jnp.maximum(jnp.sum(e, axis=-1, keepdims=True), 1e-30)
    o = lax.dot_general(p.astype(BF16), cv.astype(BF16), (((1,), (1,)), ((), ())), preferred_element_type=F32)
    oc_ref[0] = _fold_groups(jnp.where(live, o, 0.0))
    gi = lax.broadcasted_iota(I32, (8, N_HEADS), 0)
    gh = lax.broadcasted_iota(I32, (8, N_HEADS), 1) // GROUP
    imp = jnp.dot(jnp.where(gi == gh, 1.0, 0.0), p, precision=HIGHEST, preferred_element_type=F32)
    sc = jnp.dot(imp, amat_ref[...], precision=HIGHEST, preferred_element_type=F32)
    cur = past // SLC_BLOCK
    sb = lax.broadcasted_iota(I32, sc.shape, 1)
    sc = jnp.where((sb >= 1) & (sb <= cur - 2), sc, SCORE_TAKEN)
    _, firsts = _top_mask(sc, N_SEL - 3, 1)
    lane = lax.broadcasted_iota(I32, (8, LANES), 1)
    idx = jnp.where(lane == 1, cur - 1, 0)
    for n, f in enumerate(firsts):
        idx = jnp.where(lane == 2 + n, f.astype(I32), idx)
    idx_ref[0] = idx


def _s_cmp(ps, lastf, new_ckv, q3, wlt, post, bdt, amat, slopes, past):
    b = ps.shape[0]
    full = lambda a: pl.BlockSpec(a.shape, lambda i: (0,) * a.ndim)
    per = lambda a: pl.BlockSpec((1,) + a.shape[1:], lambda i: (i,) + (0,) * (a.ndim - 1))
    return pl.pallas_call(
        functools.partial(_s_cmp_kernel, past=past),
        grid=(b,),
        in_specs=[per(ps), per(lastf), per(new_ckv), per(q3), full(wlt), full(post)] + [full(w) for w in bdt]
                 + [full(amat), full(slopes)],
        out_specs=[pl.BlockSpec((1, N_HEADS, HEAD_DIM), lambda i: (i, 0, 0)), pl.BlockSpec((1, 8, LANES), lambda i: (i, 0, 0))],
        out_shape=[jax.ShapeDtypeStruct((b, N_HEADS, HEAD_DIM), F32), jax.ShapeDtypeStruct((b, 8, LANES), I32)],
        compiler_params=_cparams(("parallel",)),
        name="sample_cmp",
    )(ps, lastf, new_ckv, q3, wlt, post, *bdt, amat, slopes)


N_PAST_SEL = N_SEL - 1


def _s_slc_kernel(idx_ref, pt_ref, *refs, past):
    pages = refs[:N_PAST_SEL]
    new_ref, q_ref, slope_ref, o_ref = refs[N_PAST_SEL:]
    i = pl.program_id(0)
    kvh = pl.program_id(1)
    half = KVW // 2
    kt = jnp.concatenate([pg[0, 0:half, :] for pg in pages], axis=1)
    vt = jnp.concatenate([pg[0, half:, :] for pg in pages], axis=1)
    rep = q_ref[0, 0] * (HEAD_DIM ** -0.5)
    live = (lax.broadcasted_iota(I32, rep.shape, 1) // HEAD_DIM) == kvh
    qpad = jnp.where(live, rep, 0.0)
    s = jnp.dot(qpad.astype(BF16), kt.astype(BF16), preferred_element_type=F32)
    nk = N_PAST_SEL * PAGE_SIZE
    col = lax.broadcasted_iota(I32, (1, nk), 1)
    blk = jnp.zeros((1, nk), I32)
    for n in range(N_PAST_SEL):
        blk = blk + jnp.where(col // PAGE_SIZE == n, idx_ref[(i * N_KV_HEADS + kvh) * N_SEL + n], 0)
    row = col % PAGE_SIZE
    bpp = PAGE_SIZE // SLC_BLOCK
    valid = row // SLC_BLOCK == blk % bpp
    pos = blk * SLC_BLOCK + row % SLC_BLOCK
    slope = slope_ref[0]
    s = jnp.where(valid, s - slope * (past - pos).astype(F32), NEG)
    new = new_ref[0]
    s_new = jnp.sum(qpad * new[:, 0:half], axis=-1, keepdims=True)
    m = jnp.maximum(jnp.max(s, axis=-1, keepdims=True), s_new)
    e = jnp.exp(s - m)
    e_new = jnp.exp(s_new - m)
    l = jnp.sum(e, axis=-1, keepdims=True) + e_new
    o = lax.dot_general(e.astype(BF16), vt.astype(BF16), (((1,), (1,)), ((), ())), preferred_element_type=F32)
    o_ref[0, 0] = jnp.where(live, (o + e_new * new[:, half:]) / l, 0.0)


def _s_slc(idx, page_table, cache_t, new_skv, q4, slopes4, past):
    b, n_pages = page_table.shape
    bpp = PAGE_SIZE // SLC_BLOCK
    idx = idx.reshape(-1)
    page_table = page_table.reshape(-1)

    def blk_spec(n):
        def imap(i, k, idx_r, pt_r):
            lb = idx_r[(i * N_KV_HEADS + k) * N_SEL + n]
            return (pt_r[i * n_pages + lb // bpp], 0, 0)
        return pl.BlockSpec((1, KVW, PAGE_SIZE), imap)

    grid_spec = pltpu.PrefetchScalarGridSpec(
        num_scalar_prefetch=2,
        grid=(b, N_KV_HEADS),
        in_specs=[blk_spec(n) for n in range(N_PAST_SEL)] + [
            pl.BlockSpec((1, 1, KVW), lambda i, k, a, c: (i, 0, 0)),
            pl.BlockSpec((1, 1, GROUP, KVW // 2), lambda i, k, a, c: (i, k, 0, 0)),
            pl.BlockSpec((1, GROUP, 1), lambda i, k, a, c: (k, 0, 0)),
        ],
        out_specs=pl.BlockSpec((1, 1, GROUP, KVW // 2), lambda i, k, a, c: (i, k, 0, 0)),
    )
    return pl.pallas_call(
        functools.partial(_s_slc_kernel, past=past),
        grid_spec=grid_spec,
        out_shape=jax.ShapeDtypeStruct((b, N_KV_HEADS, GROUP, KVW // 2), F32),
        compiler_params=_cparams(("parallel", "arbitrary")),
        name="sample_slc",
    )(idx, page_table, *([cache_t] * N_PAST_SEL), new_skv, q4, slopes4)


def _s_win_kernel(st_ref, new_ref, q_ref, oc_ref, os_ref, g_ref, slopes_ref, o_ref):
    half = KVW // 2
    st = st_ref[0]
    n_buf = st.shape[1]
    qpad, live = _head_rows_q(q_ref[0])
    s = jnp.dot(qpad.astype(BF16), st[0:half].astype(BF16), preferred_element_type=F32)
    dist = n_buf - lax.broadcasted_iota(I32, (1, n_buf), 1)
    s = s - slopes_ref[...] * dist.astype(F32)
    new = new_ref[0]
    s_new = jnp.sum(qpad * new[:, 0:half], axis=-1, keepdims=True)
    m = jnp.maximum(jnp.max(s, axis=-1, keepdims=True), s_new)
    e = jnp.exp(s - m)
    e_new = jnp.exp(s_new - m)
    l = jnp.sum(e, axis=-1, keepdims=True) + e_new
    o = lax.dot_general(e.astype(BF16), st[half:].astype(BF16), (((1,), (1,)), ((), ())), preferred_element_type=F32)
    o_w = _fold_groups(jnp.where(live, (o + e_new * new[:, half:]) / l, 0.0))
    o_s = _fold_groups(os_ref[0])
    g = g_ref[0]
    o_ref[0] = g[:, 0:1] * oc_ref[0] + g[:, 1:2] * o_s + g[:, 2:3] * o_w


def _s_win(state, new_wkv, q3, oc, os16, gates, slopes):
    b = state.shape[0]
    per = lambda a: pl.BlockSpec((1,) + a.shape[1:], lambda i: (i,) + (0,) * (a.ndim - 1))
    return pl.pallas_call(
        _s_win_kernel,
        grid=(b,),
        in_specs=[per(state), per(new_wkv), per(q3), per(oc), per(os16), per(gates),
                  pl.BlockSpec(slopes.shape, lambda i: (0, 0))],
        out_specs=pl.BlockSpec((1, N_HEADS, HEAD_DIM), lambda i: (i, 0, 0)),
        out_shape=jax.ShapeDtypeStruct((b, N_HEADS, HEAD_DIM), F32),
        compiler_params=_cparams(("parallel",)),
        name="sample_win",
    )(state, new_wkv, q3, oc, os16, gates, slopes)


def _overlap_matrix(n_rows, n_cols, nsb, row_of_block0):
    a = np.zeros((n_rows, n_cols), np.float32)
    wts = (1.0, 2.0, 2.0, 2.0, 1.0)
    for sb in range(nsb):
        for d, w in enumerate(wts):
            r = SLC_RATIO * sb - 1 + d + row_of_block0
            if row_of_block0 <= r < n_rows:
                a[r, sb] = w
    return a


def _q_features():
    s = _alibi_slopes()
    parts = _bf16_parts(s)
    f = np.zeros((N_HEADS, SLC_BLOCK), np.float32)
    for i, p in enumerate(parts):
        f[:, i] = p * SLC_BLOCK
        f[:, 3 + i] = p
    f = np.broadcast_to(f[:, :, None], (N_HEADS, SLC_BLOCK, LANES))
    return np.ascontiguousarray(f).reshape(2, 2 * GROUP, SLC_BLOCK, LANES)


def _block_diag(w):
    return jnp.kron(jnp.eye(N_KV_HEADS, dtype=F32), w.astype(F32))


def kernel(x_prompt, x_sample, cache_cmp_kv, cache_slc_kv, state_win_kv, state_conv, page_table, ln1_g, w_in, cmp_pos,
           cmp_w, cmp_w1, cmp_w2, conv_w, conv_b, conv_ln_g, conv_ln_b, w_conv_pw, w_out, ln2_g, w_up, w_down, lnf_g):
    b, t, _ = x_prompt.shape
    bs, s_new, _ = x_sample.shape
    depth = ln1_g.shape[0]
    n_pages = page_table.shape[1]
    past = n_pages * PAGE_SIZE
    n_buf = state_win_kv.shape[2]
    assert depth == 1 and s_new == 1
    assert t % TQ == 0 and t // SLC_BLOCK <= SLC_BLOCK and t >= WINDOW and WINDOW % TK == 0
    assert n_pages % PAGES_PER_STEP == 0 and past // SLC_BLOCK >= N_SEL and n_buf == WINDOW and past >= WINDOW
    l = 0
    nc = t // CMP_STRIDE
    nsb = t // SLC_BLOCK
    slopes = _alibi_slopes()

    o_ng = 2 * D_CONV + D_MODEL + 3 * KVW
    wi = w_in[l]
    wm = jnp.concatenate([wi[:, :o_ng], wi[:, o_ng + 3 * N_HEADS:]], axis=1).astype(BF16)
    wg = jnp.pad(wi[:, o_ng:o_ng + 3 * N_HEADS], ((0, 0), (0, LANES - 3 * N_HEADS))).astype(BF16)
    wpw = w_conv_pw[l].astype(BF16)
    wo = w_out[l].astype(BF16)
    wu = w_up[l].astype(BF16)
    wd = w_down[l].astype(BF16)
    row = lambda v: v.reshape(1, -1).astype(F32)
    cw = jnp.pad(conv_w[l].astype(F32), ((0, 1), (0, 0)))
    lane_j = np.repeat(np.arange(2), KVW // 2)
    wl = cmp_w[l].astype(F32)[:, lane_j]
    posl = jnp.tile(cmp_pos[l].astype(F32)[:, :, None, :], (1, 1, N_KV_HEADS, 1)).reshape(CMP_LEN, KVW)
    bd = [_block_diag(cmp_w1[l, 0]), _block_diag(cmp_w2[l, 0]), _block_diag(cmp_w1[l, 1]), _block_diag(cmp_w2[l, 1])]

    x2d = x_prompt.reshape(b * t, D_MODEL)
    u, q, ckv, skv, wkv, ng, gc, ga, ckv_t, skv_t, wkv_t = _premix(x2d, row(ln1_g[l]), wm, wg, 256, t)
    cka, cvt = _compress(ckv.reshape(b, t, KVW), wl, posl, bd, nc)
    ksa, vst = _prep(skv, b, t, True)
    kwa, vwt = _prep(wkv, b, t, False)
    attn = _attention(q, ng, cka, cvt, ksa, vst, kwa, vwt, jnp.asarray(_q_features()), b, t, nc, nsb)
    conv = _conv_prompt(u, cw, row(conv_b[l]), row(conv_ln_g[l]), row(conv_ln_b[l]), wpw, b, t, 256)
    y_prompt = _post(x2d, conv, attn, gc, ga, wo, row(ln2_g[l]), wu, wd, row(lnf_g), 256).reshape(b, t, D_MODEL)

    kv_shape = (2, N_KV_HEADS, HEAD_DIM)
    rows_major = lambda a: jnp.transpose(a.reshape((b,) + kv_shape + (a.shape[-1],)), (0, 4, 1, 2, 3))[None]
    cmp_p = rows_major(ckv_t)
    slc_p = rows_major(skv_t)
    win_p = rows_major(wkv_t[:, :, t - min(WINDOW, t):])
    conv_p = u.reshape(b, t, D_CONV)[None, :, t - (CONV_WIDTH - 1):]

    xs2d = x_sample.reshape(bs, D_MODEL)
    us, qs, ckv_s, skv_s, wkv_s, ng_s, gc_s, ga_s = _premix(xs2d, row(ln1_g[l]), wm, wg, bs)
    st_conv = state_conv[l].astype(F32)
    conv_s = _conv_sample(jnp.transpose(st_conv, (1, 0, 2)), us, cw, row(conv_b[l]), row(conv_ln_g[l]),
                          row(conv_ln_b[l]), wpw)
    n_phys = cache_cmp_kv.shape[1]
    feat_major = lambda a: jnp.transpose(a, (0, 2, 3, 4, 1)).reshape(a.shape[0], KVW, a.shape[1])
    cw32 = cmp_w[l].astype(F32)
    ps, lastf = _pool_stream(page_table, feat_major(cache_cmp_kv[l]), _pool_matrices(cw32[:, 0]), _pool_matrices(cw32[:, 1]))
    nch = past // CMP_STRIDE
    nsb_s = past // SLC_BLOCK + 1
    nsbp = -(-nsb_s // LANES) * LANES
    amat_s = jnp.asarray(_overlap_matrix(nch + LANES, nsbp, nsb_s, 1))
    q3 = jnp.tile(qs.reshape(bs, N_HEADS, HEAD_DIM), (1, 1, N_KV_HEADS))
    slopes_col = jnp.asarray(slopes.reshape(N_HEADS, 1))
    bdt = [_block_diag(cmp_w1[l, 0].T), _block_diag(cmp_w2[l, 0].T), _block_diag(cmp_w1[l, 1].T), _block_diag(cmp_w2[l, 1].T)]
    oc_s, idx = _s_cmp(ps, lastf, ckv_s.reshape(bs, KVW, 1), q3, wl.T, posl.T, bdt, amat_s, slopes_col, past)
    os_s = _s_slc(idx[:, :N_KV_HEADS, :N_SEL], page_table, feat_major(cache_slc_kv[l]),
                  skv_s.reshape(bs, 1, KVW), q3.reshape(bs, N_KV_HEADS, GROUP, KVW // 2),
                  jnp.asarray(slopes.reshape(N_KV_HEADS, GROUP, 1)), past)
    gates = jnp.transpose(ng_s[:, :3 * N_HEADS].reshape(bs, 3, N_HEADS), (0, 2, 1))
    win_state = feat_major(state_win_kv[l])
    attn_s = _s_win(win_state, wkv_s.reshape(bs, 1, KVW), q3, oc_s, os_s.reshape(bs, N_HEADS, KVW // 2), gates, slopes_col)
    y_sample = _post(xs2d, conv_s, attn_s.reshape(bs, D_MODEL), gc_s, ga_s, wo, row(ln2_g[l]), wu, wd, row(lnf_g), bs)
    y_sample = y_sample.reshape(bs, 1, D_MODEL)

    cmp_s = ckv_s.reshape((1, bs, 1) + kv_shape)
    slc_s = skv_s.reshape((1, bs, 1) + kv_shape)
    win_s = jnp.concatenate([state_win_kv[l], wkv_s.reshape((bs, 1) + kv_shape).astype(state_win_kv.dtype)], axis=1)[None, :, 1:]
    conv_st = jnp.concatenate([st_conv, us[:, None, :]], axis=1)[None, :, 1:]
    return (y_prompt, y_sample, cmp_p, cmp_s, slc_p, slc_s, win_p, win_s, conv_p, conv_st)
```

```python
import functools

import numpy as np
import jax
import jax.numpy as jnp
from jax import lax
from jax.experimental import pallas as pl
from jax.experimental.pallas import tpu as pltpu

F32 = jnp.float32
BF16 = jnp.bfloat16
I32 = jnp.int32
HIGHEST = lax.Precision.HIGHEST

D_MODEL = 1024
N_HEADS = 16
HEAD_DIM = 64
N_KV_HEADS = 4
GROUP = N_HEADS // N_KV_HEADS
D_CONV = D_MODEL
CONV_WIDTH = 31
D_FF = 4 * D_MODEL
CMP_STRIDE = 16
CMP_LEN = 2 * CMP_STRIDE
SLC_BLOCK = 64
SLC_RATIO = SLC_BLOCK // CMP_STRIDE
N_SEL = 16
WINDOW = 512
PAGE_SIZE = 128
EPS = 1e-6
KVW = 2 * N_KV_HEADS * HEAD_DIM
LANES = 128
SUBLANES = 8
LOG2E = 1.4426950408889634
PAIR_W = 2 * HEAD_DIM
NEG = -1e30
SCORE_FORCED = 3e38
SCORE_BLOCKED = -1e38
SCORE_TAKEN = -3e38
VMEM_LIMIT = 56 * 1024 * 1024

TQ = 256
TK = 256
N_PARTS = 4
N_FEAT = 2 * N_PARTS
PAGES_PER_STEP = 16


def _cparams(sem):
    return pltpu.CompilerParams(dimension_semantics=sem, vmem_limit_bytes=VMEM_LIMIT)


def _alibi_slopes():
    h = np.arange(1, N_HEADS + 1, dtype=np.float32)
    return np.power(np.float32(2.0), -8.0 * h / N_HEADS).astype(np.float32)


def _bf16_parts(x):
    parts = []
    r = np.asarray(x, np.float64)
    for _ in range(N_PARTS):
        p = r.astype(np.float32).astype(BF16).astype(np.float64)
        parts.append(p.astype(np.float32))
        r = r - p
    return parts


def _sigmoid(x):
    return 1.0 / (1.0 + jnp.exp(-x))


def _silu(x):
    return x * _sigmoid(x)


def _key_tail(pos, lane, onehot):
    blk = pos // SLC_BLOCK
    off = pos % SLC_BLOCK
    f = lane - SLC_BLOCK
    feat = jnp.where((f >= 0) & (f < N_PARTS), blk, jnp.where((f >= N_PARTS) & (f < N_FEAT), off, 0))
    if onehot:
        feat = jnp.where(lane == blk, 1, feat)
    return feat.astype(F32).astype(BF16)


def _premix_kernel(x_ref, g_ref, wm_ref, wg_ref, u_ref, q_ref, ckv_ref, skv_ref, wkv_ref, ng_ref, gc_ref, ga_ref,
                   *kvt_refs):
    x = x_ref[...]
    r = lax.rsqrt(jnp.mean(x * x, axis=-1, keepdims=True) + EPS)
    h = (x * r * g_ref[...]).astype(BF16)

    def seg(lo, hi):
        return jnp.dot(h, wm_ref[:, lo:hi], preferred_element_type=F32)

    u2 = seg(0, 2 * D_CONV)
    u_ref[...] = u2[:, :D_CONV] * _sigmoid(u2[:, D_CONV:])
    o = 2 * D_CONV
    q_ref[...] = seg(o, o + D_MODEL)
    o += D_MODEL
    for i, kv_ref in enumerate((ckv_ref, skv_ref, wkv_ref)):
        kv = seg(o + i * KVW, o + (i + 1) * KVW)
        kv_ref[...] = kv
        if kvt_refs:
            kvt_refs[i][0] = kv.T
    o += 3 * KVW
    gc_ref[...] = _sigmoid(seg(o, o + D_MODEL))
    ga_ref[...] = _sigmoid(seg(o + D_MODEL, o + 2 * D_MODEL))
    ng_ref[...] = _sigmoid(jnp.dot(h, wg_ref[...], preferred_element_type=F32))


def _premix(x2d, ln_g, wm, wg, tm, seq_len=None):
    n = x2d.shape[0]
    row = lambda w: pl.BlockSpec((tm, w), lambda i: (i, 0))
    full = lambda a: pl.BlockSpec(a.shape, lambda i: (0,) * a.ndim, pipeline_mode=pl.Buffered(1))
    widths = (D_CONV, D_MODEL, KVW, KVW, KVW, LANES, D_MODEL, D_MODEL)
    out_specs = [row(w) for w in widths]
    out_shape = [jax.ShapeDtypeStruct((n, w), F32) for w in widths]
    if seq_len is not None:
        nt = seq_len // tm
        out_specs += [pl.BlockSpec((1, KVW, tm), lambda i: (i // nt, 0, i % nt))] * 3
        out_shape += [jax.ShapeDtypeStruct((n // seq_len, KVW, seq_len), F32)] * 3
    return pl.pallas_call(
        _premix_kernel,
        grid=(n // tm,),
        in_specs=[row(D_MODEL), full(ln_g), full(wm), full(wg)],
        out_specs=out_specs,
        out_shape=out_shape,
        compiler_params=_cparams(("parallel",)),
        name="premix",
    )(x2d, ln_g, wm, wg)


def _pool_pe(wl_ref, pos_ref):
    return jnp.sum(pos_ref[...] * wl_ref[...], axis=0, keepdims=True)


def _cmp_mlp(pooled, w1k_ref, w2k_ref, w1v_ref, w2v_ref):
    half = KVW // 2
    hk = _silu(jnp.dot(pooled[:, :half], w1k_ref[...], precision=HIGHEST, preferred_element_type=F32))
    ck = jnp.dot(hk, w2k_ref[...], precision=HIGHEST, preferred_element_type=F32)
    hv = _silu(jnp.dot(pooled[:, half:], w1v_ref[...], precision=HIGHEST, preferred_element_type=F32))
    cv = jnp.dot(hv, w2v_ref[...], precision=HIGHEST, preferred_element_type=F32)
    return ck, cv


N_LG = KVW // LANES


def _pool_chunks(lane_refs, wl_ref, n_chunks):
    firsts, seconds = [], []
    for j, ref in enumerate(lane_refs):
        lo = j * LANES
        first = jnp.zeros((n_chunks, LANES), F32)
        second = jnp.zeros((n_chunks, LANES), F32)
        for p in range(CMP_STRIDE):
            xp = ref[0, pl.ds(p, n_chunks, stride=CMP_STRIDE), :]
            first = first + wl_ref[p:p + 1, lo:lo + LANES] * xp
            second = second + wl_ref[CMP_STRIDE + p:CMP_STRIDE + p + 1, lo:lo + LANES] * xp
        firsts.append(first)
        seconds.append(second)
    return jnp.concatenate(firsts, axis=1), jnp.concatenate(seconds, axis=1)


def _compress_kernel(*refs, nc):
    lane_refs = refs[:N_LG]
    wl_ref, pos_ref, w1k_ref, w2k_ref, w1v_ref, w2v_ref, cka_ref, cvt_ref = refs[N_LG:]
    first, second = _pool_chunks(lane_refs, wl_ref, nc)
    nxt = pltpu.roll(second, nc - 1, axis=0)
    rows = lax.broadcasted_iota(I32, (nc, 1), 0)
    nxt = jnp.where(rows == nc - 1, 0.0, nxt)
    pooled = first + nxt + _pool_pe(wl_ref, pos_ref)
    ck, cv = _cmp_mlp(pooled, w1k_ref, w2k_ref, w1v_ref, w2v_ref)
    cend = lax.broadcasted_iota(I32, (nc, LANES), 0) * CMP_STRIDE + (CMP_LEN - 1)
    tail = _key_tail(cend, lax.broadcasted_iota(I32, (nc, LANES), 1), False)
    for p in range(2):
        cka_ref[0, p, :, 0:PAIR_W] = ck[:, p * PAIR_W:(p + 1) * PAIR_W].astype(BF16)
        cka_ref[0, p, :, PAIR_W:] = tail
        cvt_ref[0, p] = cv[:, p * PAIR_W:(p + 1) * PAIR_W].T.astype(BF16)


def _compress(ckv3, wl, posl, bd, nc):
    b, t, _ = ckv3.shape
    full = lambda a: pl.BlockSpec(a.shape, lambda i: (0,) * a.ndim)
    lane_spec = lambda j: pl.BlockSpec((1, t, LANES), lambda i: (i, 0, j))
    return pl.pallas_call(
        functools.partial(_compress_kernel, nc=nc),
        grid=(b,),
        in_specs=[lane_spec(j) for j in range(N_LG)] + [full(wl), full(posl)] + [full(w) for w in bd],
        out_specs=[pl.BlockSpec((1, 2, nc, 2 * PAIR_W), lambda i: (i, 0, 0, 0)),
                   pl.BlockSpec((1, 2, PAIR_W, nc), lambda i: (i, 0, 0, 0))],
        out_shape=[jax.ShapeDtypeStruct((b, 2, nc, 2 * PAIR_W), BF16), jax.ShapeDtypeStruct((b, 2, PAIR_W, nc), BF16)],
        compiler_params=_cparams(("parallel",)),
        name="compress",
    )(*([ckv3] * N_LG), wl, posl, *bd)


VT_ROWS = PAIR_W + 16


def _prep_kernel(x_ref, ka_ref, vt_ref, *, onehot):
    ti = pl.program_id(1)
    x = x_ref[...]
    pos = ti * TK + lax.broadcasted_iota(I32, (TK, LANES), 0)
    tail = _key_tail(pos, lax.broadcasted_iota(I32, (TK, LANES), 1), onehot)
    half = KVW // 2
    for p in range(2):
        ka_ref[0, p, :, 0:PAIR_W] = x[:, p * PAIR_W:(p + 1) * PAIR_W].astype(BF16)
        ka_ref[0, p, :, PAIR_W:] = tail
        vt_ref[0, p, 0, 0:PAIR_W, :] = x[:, half + p * PAIR_W:half + (p + 1) * PAIR_W].T.astype(BF16)
        vt_ref[0, p, 0, PAIR_W:, :] = jnp.ones((VT_ROWS - PAIR_W, TK), BF16)


def _prep(kv2d, b, t, onehot):
    nt = t // TK
    return pl.pallas_call(
        functools.partial(_prep_kernel, onehot=onehot),
        grid=(b, nt),
        in_specs=[pl.BlockSpec((TK, KVW), lambda i, j: (i * nt + j, 0))],
        out_specs=[pl.BlockSpec((1, 2, TK, 2 * PAIR_W), lambda i, j: (i, 0, j, 0)),
                   pl.BlockSpec((1, 2, 1, VT_ROWS, TK), lambda i, j: (i, 0, j, 0, 0))],
        out_shape=[jax.ShapeDtypeStruct((b, 2, t, 2 * PAIR_W), BF16),
                   jax.ShapeDtypeStruct((b, 2, nt, VT_ROWS, TK), BF16)],
        compiler_params=_cparams(("parallel", "parallel")),
        name="prep_onehot" if onehot else "prep_window",
    )(kv2d)


def _top_mask(score, n_pick, axis):
    w = score.shape[axis]
    idx = lax.broadcasted_iota(I32, score.shape, axis).astype(F32)
    taken = jnp.zeros(score.shape, F32)
    firsts = []
    for _ in range(n_pick):
        m = jnp.max(score, axis=axis, keepdims=True)
        first = jnp.min(jnp.where(score == m, idx, float(w)), axis=axis, keepdims=True)
        pick = idx == first
        taken = jnp.where(pick, 1.0, taken)
        score = jnp.where(pick, SCORE_TAKEN, score)
        firsts.append(first)
    return taken > 0.5, firsts


def _attn_kernel(q_ref, ng_ref, cka_ref, cvt_ref, ksa_ref, vst_ref, kwa_ref, vwt_ref, qfeat_ref,
                 o_ref, q8_ref, m_ref, acc_ref, oc_ref, os_ref, imp_ref, ngt_ref, *, nc, nsb):
    pair = pl.program_id(1)
    qi = pl.program_id(2)
    t0 = qi * TQ
    hp = 2 * GROUP
    cols = hp * TQ
    nlh = TQ // LANES
    qoff = lax.broadcasted_iota(I32, (1, TQ), 1)
    tq = t0 + qoff
    heads = [slice(g * TQ, (g + 1) * TQ) for g in range(hp)]

    zero_head = jnp.zeros((HEAD_DIM, TQ), F32)
    for j in range(hp // 2):
        qt = q_ref[:, LANES * j:LANES * (j + 1)].T * (HEAD_DIM ** -0.5 * LOG2E)
        for g in (2 * j, 2 * j + 1):
            c = slice(g * TQ, (g + 1) * TQ)
            own = qt[HEAD_DIM * (g % 2):HEAD_DIM * (g % 2 + 1), :]
            kpart = [own, zero_head] if g // GROUP == 0 else [zero_head, own]
            q8_ref[0:PAIR_W, c] = jnp.concatenate(kpart, axis=0).astype(BF16)
            q8_ref[PAIR_W:PAIR_W + SLC_BLOCK, c] = jnp.zeros((SLC_BLOCK, TQ), BF16)
            q8_ref[PAIR_W + SLC_BLOCK:, c] = jnp.concatenate([qfeat_ref[0, g]] * nlh, axis=1).astype(BF16)

    cend = lax.broadcasted_iota(I32, (nc, 1), 0) * CMP_STRIDE + (CMP_LEN - 1)
    tq_all = t0 + (lax.broadcasted_iota(I32, (1, cols), 1) & (TQ - 1))
    s = jnp.dot(cka_ref[0, 0], q8_ref[...], preferred_element_type=F32)
    s = jnp.where(cend <= tq_all, s, NEG)
    m = jnp.max(s, axis=0, keepdims=True)
    e = jnp.exp2(s - m)
    l = jnp.sum(e, axis=0, keepdims=True)
    p = e * jnp.where(m > 0.5 * NEG, 1.0 / l, 0.0)
    oc_ref[...] = jnp.dot(cvt_ref[0, 0], p.astype(BF16), preferred_element_type=F32)
    for kh in range(2):
        imp = p[:, heads[kh * GROUP]]
        for g in range(kh * GROUP + 1, (kh + 1) * GROUP):
            imp = imp + p[:, heads[g]]
        for h in range(nlh):
            imp_ref[kh, h] = imp[:, h * LANES:(h + 1) * LANES]

    sblk = lax.broadcasted_iota(I32, (SLC_BLOCK, 1), 0)
    cur = tq // SLC_BLOCK
    forced = (sblk == 0) | (sblk == cur) | (sblk == cur - 1)
    first_row = lax.broadcasted_iota(I32, (nsb, 1), 0) == 0
    for kh in range(2):
        halves = []
        for h in range(nlh):
            x = [imp_ref[kh, h, pl.ds(k, nsb, stride=SLC_RATIO), :] for k in range(SLC_RATIO)]
            prev = jnp.where(first_row, 0.0, pltpu.roll(x[SLC_RATIO - 1], 1, axis=0))
            halves.append(prev + 2.0 * (x[0] + x[1] + x[2]) + x[3])
        sc = jnp.concatenate(halves, axis=1)
        if nsb < SLC_BLOCK:
            sc = jnp.concatenate([sc, jnp.zeros((SLC_BLOCK - nsb, TQ), F32)], axis=0)
        sc = jnp.where(forced, SCORE_FORCED, jnp.where(sblk > cur, SCORE_BLOCKED, sc))
        sel, _ = _top_mask(sc, min(N_SEL, nsb), 0)
        bias = jnp.where(sel & (sblk <= cur), 0.0, NEG).astype(BF16)
        for g in range(kh * GROUP, (kh + 1) * GROUP):
            q8_ref[PAIR_W:PAIR_W + SLC_BLOCK, g * TQ:(g + 1) * TQ] = bias

    def start():
        m_ref[...] = jnp.full(m_ref.shape, NEG, F32)
        acc_ref[...] = jnp.zeros(acc_ref.shape, F32)

    def online(ka_ref, vt_ref, kt, mask, nt=1):
        keys = ka_ref[0, 0, pl.ds(pl.multiple_of(kt * TK, TK), nt * TK), :]
        s = jnp.dot(keys, q8_ref[...], preferred_element_type=F32)
        if mask is not None:
            s = jnp.where(mask, s, NEG)
        m_prev = m_ref[0:1, :]
        m_new = jnp.maximum(m_prev, jnp.max(s, axis=0, keepdims=True))
        alpha = jnp.exp2(m_prev - m_new)
        p = jnp.exp2(s - m_new).astype(BF16)
        acc = alpha * acc_ref[...]
        for i in range(nt):
            acc = acc + jnp.dot(vt_ref[0, 0, kt + i], p[i * TK:(i + 1) * TK], preferred_element_type=F32)
        acc_ref[...] = acc
        m_ref[0:1, :] = m_new

    def finish():
        return acc_ref[0:PAIR_W, :] / acc_ref[PAIR_W:PAIR_W + 1, :]

    krow = lax.broadcasted_iota(I32, (TK, 1), 0)
    qoff8 = lax.broadcasted_iota(I32, (1, cols), 1) & (TQ - 1)
    causal = krow <= qoff8

    start()

    def body(k2, carry):
        online(ksa_ref, vst_ref, 2 * k2, None, 2)
        return carry

    lax.fori_loop(0, qi // 2, body, 0)

    @pl.when(qi % 2 == 1)
    def _():
        online(ksa_ref, vst_ref, qi - 1, None)

    online(ksa_ref, vst_ref, qi, causal)
    os_ref[...] = finish()

    start()
    nwt = WINDOW // TK + 1
    for i in range(nwt - 1):
        back = nwt - 1 - i

        @pl.when(qi >= back)
        def _():
            online(kwa_ref, vwt_ref, qi - back, (krow >= qoff8) if i == 0 else None)
    online(kwa_ref, vwt_ref, qi, causal)
    o_w = finish()

    ngt_ref[...] = ng_ref[...].T
    outs = []
    for g in range(hp):
        kh = g // GROUP
        h = pair * hp + g
        gate = [ngt_ref[pl.ds(j * N_HEADS + h, 1), :] for j in range(3)]
        r = slice(kh * HEAD_DIM, (kh + 1) * HEAD_DIM)
        c = slice(g * TQ, (g + 1) * TQ)
        outs.append(gate[0] * oc_ref[r, c] + gate[1] * os_ref[r, c] + gate[2] * o_w[r, c])
    o_ref[...] = jnp.concatenate(outs, axis=0).T


def _attention(q2d, ng, cka, cvt, ksa, vst, kwa, vwt, qfeat, b, t, nc, nsb):
    nq = t // TQ
    hp = 2 * GROUP
    cols = hp * TQ
    pw = hp * HEAD_DIM
    per_pair = lambda a: pl.BlockSpec((1, 1) + a.shape[2:], lambda i, p, j: (i, p) + (0,) * (a.ndim - 2))
    return pl.pallas_call(
        functools.partial(_attn_kernel, nc=nc, nsb=nsb),
        grid=(b, 2, nq),
        in_specs=[
            pl.BlockSpec((TQ, pw), lambda i, p, j: (i * nq + j, p)),
            pl.BlockSpec((TQ, LANES), lambda i, p, j: (i * nq + j, 0)),
            per_pair(cka), per_pair(cvt), per_pair(ksa), per_pair(vst), per_pair(kwa), per_pair(vwt),
            pl.BlockSpec((1,) + qfeat.shape[1:], lambda i, p, j: (p, 0, 0, 0)),
        ],
        out_specs=pl.BlockSpec((TQ, pw), lambda i, p, j: (i * nq + j, p)),
        out_shape=jax.ShapeDtypeStruct((b * t, N_HEADS * HEAD_DIM), F32),
        scratch_shapes=[
            pltpu.VMEM((2 * PAIR_W, cols), BF16),
            pltpu.VMEM((8, cols), F32),
            pltpu.VMEM((VT_ROWS, cols), F32),
            pltpu.VMEM((PAIR_W, cols), F32),
            pltpu.VMEM((PAIR_W, cols), F32),
            pltpu.VMEM((2, TQ // LANES, nc, LANES), F32),
            pltpu.VMEM((LANES, TQ), F32),
        ],
        compiler_params=_cparams(("parallel", "parallel", "arbitrary")),
        name="attn",
    )(q2d, ng, cka, cvt, ksa, vst, kwa, vwt, qfeat)


HALO = 32


def _ln_silu_pw(y, lg_ref, lb_ref, wpw_ref):
    mu = jnp.mean(y, axis=-1, keepdims=True)
    d = y - mu
    var = jnp.mean(d * d, axis=-1, keepdims=True)
    z = d * lax.rsqrt(var + EPS) * lg_ref[...] + lb_ref[...]
    return jnp.dot(_silu(z).astype(BF16), wpw_ref[...], preferred_element_type=F32)


def _conv_kernel(prev_ref, cur_ref, cw_ref, cb_ref, lg_ref, lb_ref, wpw_ref, o_ref, ext_ref, *, tm):
    first = pl.program_id(1) == 0
    ext_ref[0:HALO, :] = jnp.where(first, 0.0, prev_ref[...])
    ext_ref[HALO:HALO + tm, :] = cur_ref[...]
    ext_ref[HALO + tm:, :] = jnp.zeros((SUBLANES, D_CONV), F32)
    base = HALO - (CONV_WIDTH - 1)
    y = None
    for r in range(SUBLANES):
        z = None
        for a in range((base + CONV_WIDTH - 1) // SUBLANES + 1):
            k = a * SUBLANES + r - base
            if 0 <= k < CONV_WIDTH:
                term = cw_ref[k:k + 1, :] * ext_ref[a * SUBLANES:a * SUBLANES + tm + SUBLANES, :]
                z = term if z is None else z + term
        part = z[r:r + tm, :]
        y = part if y is None else y + part
    o_ref[...] = _ln_silu_pw(y + cb_ref[...], lg_ref, lb_ref, wpw_ref)


def _conv_prompt(u2d, cw, cb, lg, lb, wpw, b, t, tm):
    nt = t // tm
    full = lambda a: pl.BlockSpec(a.shape, lambda i, j: (0,) * a.ndim)
    return pl.pallas_call(
        functools.partial(_conv_kernel, tm=tm),
        grid=(b, nt),
        in_specs=[pl.BlockSpec((HALO, D_CONV), lambda i, j: (jnp.maximum((i * nt + j) * (tm // HALO) - 1, 0), 0)),
                  pl.BlockSpec((tm, D_CONV), lambda i, j: (i * nt + j, 0)),
                  full(cw), full(cb), full(lg), full(lb), full(wpw)],
        out_specs=pl.BlockSpec((tm, D_MODEL), lambda i, j: (i * nt + j, 0)),
        out_shape=jax.ShapeDtypeStruct((b * t, D_MODEL), F32),
        scratch_shapes=[pltpu.VMEM((HALO + tm + SUBLANES, D_CONV), F32)],
        compiler_params=_cparams(("parallel", "arbitrary")),
        name="conv",
    )(u2d, u2d, cw, cb, lg, lb, wpw)


def _conv_sample_kernel(st_ref, u_ref, cw_ref, cb_ref, lg_ref, lb_ref, wpw_ref, o_ref):
    y = cw_ref[CONV_WIDTH - 1:CONV_WIDTH, :] * u_ref[...]
    for k in range(CONV_WIDTH - 1):
        y = y + cw_ref[k:k + 1, :] * st_ref[k]
    o_ref[...] = _ln_silu_pw(y + cb_ref[...], lg_ref, lb_ref, wpw_ref)


def _conv_sample(state_t, u, cw, cb, lg, lb, wpw):
    args = (state_t, u, cw, cb, lg, lb, wpw)
    return pl.pallas_call(
        _conv_sample_kernel,
        out_shape=jax.ShapeDtypeStruct(u.shape, F32),
        compiler_params=pltpu.CompilerParams(vmem_limit_bytes=VMEM_LIMIT),
        name="conv_sample",
    )(*args)


def _post_kernel(x_ref, c_ref, a_ref, gc_ref, ga_ref, wo_ref, g2_ref, wu_ref, wd_ref, gf_ref, y_ref):
    m = gc_ref[...] * c_ref[...] + ga_ref[...] * a_ref[...]
    x1 = x_ref[...] + jnp.dot(m.astype(BF16), wo_ref[...], preferred_element_type=F32)
    r = lax.rsqrt(jnp.mean(x1 * x1, axis=-1, keepdims=True) + EPS)
    h = (x1 * r * g2_ref[...]).astype(BF16)
    f = jnp.maximum(jnp.dot(h, wu_ref[...], preferred_element_type=F32), 0.0)
    x2 = x1 + jnp.dot((f * f).astype(BF16), wd_ref[...], preferred_element_type=F32)
    r2 = lax.rsqrt(jnp.mean(x2 * x2, axis=-1, keepdims=True) + EPS)
    y_ref[...] = x2 * r2 * gf_ref[...]


def _post(x2d, conv, attn, gc, ga, wo, g2, wu, wd, gf, tm):
    n = x2d.shape[0]
    row = pl.BlockSpec((tm, D_MODEL), lambda i: (i, 0))
    full = lambda a: pl.BlockSpec(a.shape, lambda i: (0,) * a.ndim, pipeline_mode=pl.Buffered(1))
    return pl.pallas_call(
        _post_kernel,
        grid=(n // tm,),
        in_specs=[row] * 5 + [full(wo), full(g2), full(wu), full(wd), full(gf)],
        out_specs=row,
        out_shape=jax.ShapeDtypeStruct((n, D_MODEL), F32),
        compiler_params=_cparams(("parallel",)),
        name="post",
    )(x2d, conv, attn, gc, ga, wo, g2, wu, wd, gf)


CHUNKS_PER_PAGE = PAGE_SIZE // CMP_STRIDE
STEP_CHUNKS = PAGES_PER_STEP * CHUNKS_PER_PAGE


def _pool_stream_kernel(pt_ref, *refs):
    npg = PAGES_PER_STEP
    pages = refs[:npg]
    mk_ref, mv_ref, out_ref, lastf_ref, carry_ref = refs[npg:]
    half = KVW // 2

    @pl.when(pl.program_id(1) == 0)
    def _():
        carry_ref[...] = jnp.zeros(carry_ref.shape, F32)

    acc_k = jnp.zeros((half, 2 * STEP_CHUNKS), F32)
    acc_v = jnp.zeros((half, 2 * STEP_CHUNKS), F32)
    for g in range(npg):
        x = pages[g][0]
        hi = x.astype(BF16)
        lo = (x - hi.astype(F32)).astype(BF16)
        xx = jnp.concatenate([hi, lo], axis=1)
        acc_k = acc_k + jnp.dot(xx[0:half], mk_ref[g], preferred_element_type=F32)
        acc_v = acc_v + jnp.dot(xx[half:], mv_ref[g], preferred_element_type=F32)
    full = jnp.concatenate([acc_k, acc_v], axis=0)
    lane = lax.broadcasted_iota(I32, (KVW, STEP_CHUNKS), 1)
    out_ref[0] = full[:, 0:STEP_CHUNKS] + jnp.where(lane == 0, carry_ref[...], 0.0)
    carry_ref[...] = full[:, STEP_CHUNKS:]
    lastf_ref[0] = full[:, STEP_CHUNKS:]


def _pool_matrices(w):
    pos = np.arange(2 * PAGE_SIZE) % PAGE_SIZE
    chunk, p = pos // CMP_STRIDE, pos % CMP_STRIDE
    col = np.arange(2 * STEP_CHUNKS)
    mats = []
    for g in range(PAGES_PER_STEP):
        c0 = g * CHUNKS_PER_PAGE + chunk
        second = (col[None, :] == c0[:, None]).astype(np.float32)
        first = (col[None, :] == c0[:, None] + 1).astype(np.float32)
        mats.append(w[CMP_STRIDE + p][:, None] * second + w[p][:, None] * first)
    return jnp.stack(mats).astype(BF16)


def _pool_stream(page_table, cache_t, mk, mv):
    b, n_pages = page_table.shape
    npg = PAGES_PER_STEP
    page_spec = lambda j: pl.BlockSpec((1, KVW, PAGE_SIZE), lambda i, g, pt: (pt[i, g * npg + j], 0, 0))
    const = lambda a: pl.BlockSpec(a.shape, lambda i, g, pt: (0,) * a.ndim)
    grid_spec = pltpu.PrefetchScalarGridSpec(
        num_scalar_prefetch=1,
        grid=(b, n_pages // npg),
        in_specs=[page_spec(j) for j in range(npg)] + [const(mk), const(mv)],
        out_specs=[pl.BlockSpec((1, KVW, STEP_CHUNKS), lambda i, g, pt: (i, 0, g)),
                   pl.BlockSpec((1, KVW, STEP_CHUNKS), lambda i, g, pt: (i, 0, 0))],
        scratch_shapes=[pltpu.VMEM((KVW, STEP_CHUNKS), F32)],
    )
    n_chunks = n_pages * CHUNKS_PER_PAGE
    return pl.pallas_call(
        _pool_stream_kernel,
        grid_spec=grid_spec,
        out_shape=[jax.ShapeDtypeStruct((b, KVW, n_chunks), F32), jax.ShapeDtypeStruct((b, KVW, STEP_CHUNKS), F32)],
        compiler_params=_cparams(("parallel", "arbitrary")),
        name="pool_stream",
    )(page_table, *([cache_t] * npg), mk, mv)


def _head_rows_q(qrep):
    rep = qrep * (HEAD_DIM ** -0.5)
    lane_kv = lax.broadcasted_iota(I32, rep.shape, 1) // HEAD_DIM
    row_kv = lax.broadcasted_iota(I32, rep.shape, 0) // GROUP
    return jnp.where(lane_kv == row_kv, rep, 0.0), lane_kv == row_kv


def _fold_groups(x):
    return x[:, 0:64] + x[:, 64:128] + x[:, 128:192] + x[:, 192:256]


def _s_cmp_kernel(ps_ref, lastf_ref, new_ref, q_ref, wlt_ref, post_ref, w1k_ref, w2k_ref, w1v_ref, w2v_ref, amat_ref,
                  slopes_ref, oc_ref, idx_ref, *, past):
    nch = past // CMP_STRIDE
    r = nch + LANES
    half = KVW // 2
    new = new_ref[0]
    tl = lax.broadcasted_iota(I32, (KVW, LANES), 1)
    tail = jnp.where(tl == 0, lastf_ref[0, :, 0:1] + wlt_ref[:, CMP_STRIDE:CMP_STRIDE + 1] * new,
                     jnp.where(tl == 1, wlt_ref[:, 0:1] * new, 0.0))
    pe = jnp.sum(post_ref[...] * wlt_ref[...], axis=1, keepdims=True)
    pooled = jnp.concatenate([ps_ref[0], tail], axis=1) + pe
    hk = _silu(jnp.dot(w1k_ref[...], pooled[0:half], precision=HIGHEST, preferred_element_type=F32))
    ck = jnp.dot(w2k_ref[...], hk, precision=HIGHEST, preferred_element_type=F32)
    hv = _silu(jnp.dot(w1v_ref[...], pooled[half:], precision=HIGHEST, preferred_element_type=F32))
    cv = jnp.dot(w2v_ref[...], hv, precision=HIGHEST, preferred_element_type=F32)
    qpad, live = _head_rows_q(q_ref[0])
    s = jnp.dot(qpad, ck, precision=HIGHEST, preferred_element_type=F32)
    j = lax.broadcasted_iota(I32, (1, r), 1)
    cend = (j * CMP_STRIDE + (CMP_LEN - 1 - CMP_STRIDE))
    valid = (j >= 1) & (cend <= past)
    s = s - slopes_ref[...] * (past - cend).astype(F32)
    s = jnp.where(valid, s, NEG)
    m = jnp.max(s, axis=-1, keepdims=True)
    e = jnp.where(valid, jnp.exp(s - m), 0.0)
    p = e / jnp.maximum(jnp.sum(e, axis=-1, keepdims=True), 1e-30)
    o = lax.dot_general(p.astype(BF16), cv.astype(BF16), (((1,), (1,)), ((), ())), preferred_element_type=F32)
    oc_ref[0] = _fold_groups(jnp.where(live, o, 0.0))
    gi = lax.broadcasted_iota(I32, (8, N_HEADS), 0)
    gh = lax.broadcasted_iota(I32, (8, N_HEADS), 1) // GROUP
    imp = jnp.dot(jnp.where(gi == gh, 1.0, 0.0), p, precision=HIGHEST, preferred_element_type=F32)
    sc = jnp.dot(imp, amat_ref[...], precision=HIGHEST, preferred_element_type=F32)
    cur = past // SLC_BLOCK
    sb = lax.broadcasted_iota(I32, sc.shape, 1)
    sc = jnp.where((sb >= 1) & (sb <= cur - 2), sc, SCORE_TAKEN)
    _, firsts = _top_mask(sc, N_SEL - 3, 1)
    lane = lax.broadcasted_iota(I32, (8, LANES), 1)
    idx = jnp.where(lane == 1, cur - 1, 0)
    for n, f in enumerate(firsts):
        idx = jnp.where(lane == 2 + n, f.astype(I32), idx)
    idx_ref[0] = idx


def _s_cmp(ps, lastf, new_ckv, q3, wlt, post, bdt, amat, slopes, past):
    b = ps.shape[0]
    full = lambda a: pl.BlockSpec(a.shape, lambda i: (0,) * a.ndim)
    per = lambda a: pl.BlockSpec((1,) + a.shape[1:], lambda i: (i,) + (0,) * (a.ndim - 1))
    return pl.pallas_call(
        functools.partial(_s_cmp_kernel, past=past),
        grid=(b,),
        in_specs=[per(ps), per(lastf), per(new_ckv), per(q3), full(wlt), full(post)] + [full(w) for w in bdt]
                 + [full(amat), full(slopes)],
        out_specs=[pl.BlockSpec((1, N_HEADS, HEAD_DIM), lambda i: (i, 0, 0)), pl.BlockSpec((1, 8, LANES), lambda i: (i, 0, 0))],
        out_shape=[jax.ShapeDtypeStruct((b, N_HEADS, HEAD_DIM), F32), jax.ShapeDtypeStruct((b, 8, LANES), I32)],
        compiler_params=_cparams(("parallel",)),
        name="sample_cmp",
    )(ps, lastf, new_ckv, q3, wlt, post, *bdt, amat, slopes)


N_PAST_SEL = N_SEL - 1


def _s_slc_kernel(idx_ref, pt_ref, *refs, past):
    pages = refs[:N_PAST_SEL]
    new_ref, q_ref, slope_ref, o_ref = refs[N_PAST_SEL:]
    i = pl.program_id(0)
    kvh = pl.program_id(1)
    half = KVW // 2
    kt = jnp.concatenate([pg[0, 0:half, :] for pg in pages], axis=1)
    vt = jnp.concatenate([pg[0, half:, :] for pg in pages], axis=1)
    rep = q_ref[0, 0] * (HEAD_DIM ** -0.5)
    live = (lax.broadcasted_iota(I32, rep.shape, 1) // HEAD_DIM) == kvh
    qpad = jnp.where(live, rep, 0.0)
    s = jnp.dot(qpad.astype(BF16), kt.astype(BF16), preferred_element_type=F32)
    nk = N_PAST_SEL * PAGE_SIZE
    col = lax.broadcasted_iota(I32, (1, nk), 1)
    blk = jnp.zeros((1, nk), I32)
    for n in range(N_PAST_SEL):
        blk = blk + jnp.where(col // PAGE_SIZE == n, idx_ref[(i * N_KV_HEADS + kvh) * N_SEL + n], 0)
    row = col % PAGE_SIZE
    bpp = PAGE_SIZE // SLC_BLOCK
    valid = row // SLC_BLOCK == blk % bpp
    pos = blk * SLC_BLOCK + row % SLC_BLOCK
    slope = slope_ref[0]
    s = jnp.where(valid, s - slope * (past - pos).astype(F32), NEG)
    new = new_ref[0]
    s_new = jnp.sum(qpad * new[:, 0:half], axis=-1, keepdims=True)
    m = jnp.maximum(jnp.max(s, axis=-1, keepdims=True), s_new)
    e = jnp.exp(s - m)
    e_new = jnp.exp(s_new - m)
    l = jnp.sum(e, axis=-1, keepdims=True) + e_new
    o = lax.dot_general(e.astype(BF16), vt.astype(BF16), (((1,), (1,)), ((), ())), preferred_element_type=F32)
    o_ref[0, 0] = jnp.where(live, (o + e_new * new[:, half:]) / l, 0.0)


def _s_slc(idx, page_table, cache_t, new_skv, q4, slopes4, past):
    b, n_pages = page_table.shape
    bpp = PAGE_SIZE // SLC_BLOCK
    idx = idx.reshape(-1)
    page_table = page_table.reshape(-1)

    def blk_spec(n):
        def imap(i, k, idx_r, pt_r):
            lb = idx_r[(i * N_KV_HEADS + k) * N_SEL + n]
            return (pt_r[i * n_pages + lb // bpp], 0, 0)
        return pl.BlockSpec((1, KVW, PAGE_SIZE), imap)

    grid_spec = pltpu.PrefetchScalarGridSpec(
        num_scalar_prefetch=2,
        grid=(b, N_KV_HEADS),
        in_specs=[blk_spec(n) for n in range(N_PAST_SEL)] + [
            pl.BlockSpec((1, 1, KVW), lambda i, k, a, c: (i, 0, 0)),
            pl.BlockSpec((1, 1, GROUP, KVW // 2), lambda i, k, a, c: (i, k, 0, 0)),
            pl.BlockSpec((1, GROUP, 1), lambda i, k, a, c: (k, 0, 0)),
        ],
        out_specs=pl.BlockSpec((1, 1, GROUP, KVW // 2), lambda i, k, a, c: (i, k, 0, 0)),
    )
    return pl.pallas_call(
        functools.partial(_s_slc_kernel, past=past),
        grid_spec=grid_spec,
        out_shape=jax.ShapeDtypeStruct((b, N_KV_HEADS, GROUP, KVW // 2), F32),
        compiler_params=_cparams(("parallel", "arbitrary")),
        name="sample_slc",
    )(idx, page_table, *([cache_t] * N_PAST_SEL), new_skv, q4, slopes4)


def _s_win_kernel(st_ref, new_ref, q_ref, oc_ref, os_ref, g_ref, slopes_ref, o_ref):
    half = KVW // 2
    st = st_ref[0]
    n_buf = st.shape[1]
    qpad, live = _head_rows_q(q_ref[0])
    s = jnp.dot(qpad.astype(BF16), st[0:half].astype(BF16), preferred_element_type=F32)
    dist = n_buf - lax.broadcasted_iota(I32, (1, n_buf), 1)
    s = s - slopes_ref[...] * dist.astype(F32)
    new = new_ref[0]
    s_new = jnp.sum(qpad * new[:, 0:half], axis=-1, keepdims=True)
    m = jnp.maximum(jnp.max(s, axis=-1, keepdims=True), s_new)
    e = jnp.exp(s - m)
    e_new = jnp.exp(s_new - m)
    l = jnp.sum(e, axis=-1, keepdims=True) + e_new
    o = lax.dot_general(e.astype(BF16), st[half:].astype(BF16), (((1,), (1,)), ((), ())), preferred_element_type=F32)
    o_w = _fold_groups(jnp.where(live, (o + e_new * new[:, half:]) / l, 0.0))
    o_s = _fold_groups(os_ref[0])
    g = g_ref[0]
    o_ref[0] = g[:, 0:1] * oc_ref[0] + g[:, 1:2] * o_s + g[:, 2:3] * o_w


def _s_win(state, new_wkv, q3, oc, os16, gates, slopes):
    b = state.shape[0]
    per = lambda a: pl.BlockSpec((1,) + a.shape[1:], lambda i: (i,) + (0,) * (a.ndim - 1))
    return pl.pallas_call(
        _s_win_kernel,
        grid=(b,),
        in_specs=[per(state), per(new_wkv), per(q3), per(oc), per(os16), per(gates),
                  pl.BlockSpec(slopes.shape, lambda i: (0, 0))],
        out_specs=pl.BlockSpec((1, N_HEADS, HEAD_DIM), lambda i: (i, 0, 0)),
        out_shape=jax.ShapeDtypeStruct((b, N_HEADS, HEAD_DIM), F32),
        compiler_params=_cparams(("parallel",)),
        name="sample_win",
    )(state, new_wkv, q3, oc, os16, gates, slopes)


def _overlap_matrix(n_rows, n_cols, nsb, row_of_block0):
    a = np.zeros((n_rows, n_cols), np.float32)
    wts = (1.0, 2.0, 2.0, 2.0, 1.0)
    for sb in range(nsb):
        for d, w in enumerate(wts):
            r = SLC_RATIO * sb - 1 + d + row_of_block0
            if row_of_block0 <= r < n_rows:
                a[r, sb] = w
    return a


def _q_features():
    parts = _bf16_parts(_alibi_slopes().astype(np.float64) * LOG2E)
    f = np.zeros((N_HEADS, SLC_BLOCK), np.float32)
    for i, p in enumerate(parts):
        f[:, i] = p * SLC_BLOCK
        f[:, N_PARTS + i] = p
    f = np.broadcast_to(f[:, :, None], (N_HEADS, SLC_BLOCK, LANES))
    return np.ascontiguousarray(f).reshape(2, 2 * GROUP, SLC_BLOCK, LANES)


def _block_diag(w):
    return jnp.kron(jnp.eye(N_KV_HEADS, dtype=F32), w.astype(F32))


def kernel(x_prompt, x_sample, cache_cmp_kv, cache_slc_kv, state_win_kv, state_conv, page_table, ln1_g, w_in, cmp_pos,
           cmp_w, cmp_w1, cmp_w2, conv_w, conv_b, conv_ln_g, conv_ln_b, w_conv_pw, w_out, ln2_g, w_up, w_down, lnf_g):
    b, t, _ = x_prompt.shape
    bs, s_new, _ = x_sample.shape
    depth = ln1_g.shape[0]
    n_pages = page_table.shape[1]
    past = n_pages * PAGE_SIZE
    n_buf = state_win_kv.shape[2]
    assert depth == 1 and s_new == 1
    assert t % TQ == 0 and t // SLC_BLOCK <= SLC_BLOCK and t >= WINDOW and WINDOW % TK == 0
    assert n_pages % PAGES_PER_STEP == 0 and past // SLC_BLOCK >= N_SEL and n_buf == WINDOW and past >= WINDOW
    l = 0
    nc = t // CMP_STRIDE
    nsb = t // SLC_BLOCK
    slopes = _alibi_slopes()

    o_ng = 2 * D_CONV + D_MODEL + 3 * KVW
    wi = w_in[l]
    wm = jnp.concatenate([wi[:, :o_ng], wi[:, o_ng + 3 * N_HEADS:]], axis=1).astype(BF16)
    wg = jnp.pad(wi[:, o_ng:o_ng + 3 * N_HEADS], ((0, 0), (0, LANES - 3 * N_HEADS))).astype(BF16)
    wpw = w_conv_pw[l].astype(BF16)
    wo = w_out[l].astype(BF16)
    wu = w_up[l].astype(BF16)
    wd = w_down[l].astype(BF16)
    row = lambda v: v.reshape(1, -1).astype(F32)
    cw = jnp.pad(conv_w[l].astype(F32), ((0, 1), (0, 0)))
    lane_j = np.repeat(np.arange(2), KVW // 2)
    wl = cmp_w[l].astype(F32)[:, lane_j]
    posl = jnp.tile(cmp_pos[l].astype(F32)[:, :, None, :], (1, 1, N_KV_HEADS, 1)).reshape(CMP_LEN, KVW)
    bd = [_block_diag(cmp_w1[l, 0]), _block_diag(cmp_w2[l, 0]), _block_diag(cmp_w1[l, 1]), _block_diag(cmp_w2[l, 1])]

    x2d = x_prompt.reshape(b * t, D_MODEL)
    u, q, ckv, skv, wkv, ng, gc, ga, ckv_t, skv_t, wkv_t = _premix(x2d, row(ln1_g[l]), wm, wg, 256, t)
    cka, cvt = _compress(ckv.reshape(b, t, KVW), wl, posl, bd, nc)
    ksa, vst = _prep(skv, b, t, True)
    kwa, vwt = _prep(wkv, b, t, False)
    attn = _attention(q, ng, cka, cvt, ksa, vst, kwa, vwt, jnp.asarray(_q_features()), b, t, nc, nsb)
    conv = _conv_prompt(u, cw, row(conv_b[l]), row(conv_ln_g[l]), row(conv_ln_b[l]), wpw, b, t, 256)
    y_prompt = _post(x2d, conv, attn, gc, ga, wo, row(ln2_g[l]), wu, wd, row(lnf_g), 256).reshape(b, t, D_MODEL)

    kv_shape = (2, N_KV_HEADS, HEAD_DIM)
    rows_major = lambda a: jnp.transpose(a.reshape((b,) + kv_shape + (a.shape[-1],)), (0, 4, 1, 2, 3))[None]
    cmp_p = rows_major(ckv_t)
    slc_p = rows_major(skv_t)
    win_p = rows_major(wkv_t[:, :, t - min(WINDOW, t):])
    conv_p = u.reshape(b, t, D_CONV)[None, :, t - (CONV_WIDTH - 1):]

    xs2d = x_sample.reshape(bs, D_MODEL)
    us, qs, ckv_s, skv_s, wkv_s, ng_s, gc_s, ga_s = _premix(xs2d, row(ln1_g[l]), wm, wg, bs)
    st_conv = state_conv[l].astype(F32)
    conv_s = _conv_sample(jnp.transpose(st_conv, (1, 0, 2)), us, cw, row(conv_b[l]), row(conv_ln_g[l]),
                          row(conv_ln_b[l]), wpw)
    n_phys = cache_cmp_kv.shape[1]
    feat_major = lambda a: jnp.transpose(a, (0, 2, 3, 4, 1)).reshape(a.shape[0], KVW, a.shape[1])
    cw32 = cmp_w[l].astype(F32)
    ps, lastf = _pool_stream(page_table, feat_major(cache_cmp_kv[l]), _pool_matrices(cw32[:, 0]), _pool_matrices(cw32[:, 1]))
    nch = past // CMP_STRIDE
    nsb_s = past // SLC_BLOCK + 1
    nsbp = -(-nsb_s // LANES) * LANES
    amat_s = jnp.asarray(_overlap_matrix(nch + LANES, nsbp, nsb_s, 1))
    q3 = jnp.tile(qs.reshape(bs, N_HEADS, HEAD_DIM), (1, 1, N_KV_HEADS))
    slopes_col = jnp.asarray(slopes.reshape(N_HEADS, 1))
    bdt = [_block_diag(cmp_w1[l, 0].T), _block_diag(cmp_w2[l, 0].T), _block_diag(cmp_w1[l, 1].T), _block_diag(cmp_w2[l, 1].T)]
    oc_s, idx = _s_cmp(ps, lastf, ckv_s.reshape(bs, KVW, 1), q3, wl.T, posl.T, bdt, amat_s, slopes_col, past)
    os_s = _s_slc(idx[:, :N_KV_HEADS, :N_SEL], page_table, feat_major(cache_slc_kv[l]),
                  skv_s.reshape(bs, 1, KVW), q3.reshape(bs, N_KV_HEADS, GROUP, KVW // 2),
                  jnp.asarray(slopes.reshape(N_KV_HEADS, GROUP, 1)), past)
    gates = jnp.transpose(ng_s[:, :3 * N_HEADS].reshape(bs, 3, N_HEADS), (0, 2, 1))
    win_state = feat_major(state_win_kv[l])
    attn_s = _s_win(win_state, wkv_s.reshape(bs, 1, KVW), q3, oc_s, os_s.reshape(bs, N_HEADS, KVW // 2), gates, slopes_col)
    y_sample = _post(xs2d, conv_s, attn_s.reshape(bs, D_MODEL), gc_s, ga_s, wo, row(ln2_g[l]), wu, wd, row(lnf_g), bs)
    y_sample = y_sample.reshape(bs, 1, D_MODEL)

    cmp_s = ckv_s.reshape((1, bs, 1) + kv_shape)
    slc_s = skv_s.reshape((1, bs, 1) + kv_shape)
    win_s = jnp.concatenate([state_win_kv[l], wkv_s.reshape((bs, 1) + kv_shape).astype(state_win_kv.dtype)], axis=1)[None, :, 1:]
    conv_st = jnp.concatenate([st_conv, us[:, None, :]], axis=1)[None, :, 1:]
    return (y_prompt, y_sample, cmp_p, cmp_s, slc_p, slc_s, win_p, win_s, conv_p, conv_st)
```
